```python
import math
import jax, jax.numpy as jnp
from jax import lax
import numpy as np

D_MODEL = 1024
BATCH = 8
SEQ = 4096
DEPTH = 1
DEC_BATCH = 16
DEC_SEQ = 2048
PAST_LEN = 128

DA_HEADS = 4
DA_HEAD_DIM = 64
DA_V_DIM = 2 * DA_HEAD_DIM
DA_QK = DA_HEADS * 2 * DA_HEAD_DIM
DA_WIDTH = DA_HEADS * DA_V_DIM
GLA_HEADS = 4
GLA_DK = 64
GLA_DV = 128
GLA_QK = GLA_HEADS * GLA_DK
GLA_WIDTH = GLA_HEADS * GLA_DV
GLA_GATE_RANK = 16
GLA_GATE_TAU = 16.0
GLA_CHUNK = 64
N_BRANCHES = 2
REL_BUCKETS = 32
REL_MAX_DIST = 128
Q_BLOCK = 128
PEER_HEADS = 8
PEER_N_KEYS = 128
PEER_N_EXPERTS = PEER_N_KEYS * PEER_N_KEYS
PEER_TOPK = 16
PEER_QUERY_DIM = 256
PEER_HALF = PEER_QUERY_DIM // 2
PEER_TOKEN_BLOCK = 128
PLE_DIM = 256
EPS = 1e-6

IN_SIZES = (DA_QK, DA_QK, DA_WIDTH, GLA_QK, GLA_QK, GLA_WIDTH, GLA_WIDTH, 2 * GLA_GATE_RANK, N_BRANCHES * D_MODEL)
IN_DIM = DA_QK * 2 + DA_WIDTH + GLA_QK * 2 + GLA_WIDTH * 2 + 2 * GLA_GATE_RANK + N_BRANCHES * D_MODEL

kernel_name = "hybrid_diffattn_gla_peer_encoder"


def rmsnorm(x, g):
    xf = x.astype(jnp.float32)
    y = xf * lax.rsqrt(jnp.mean(xf * xf, axis=-1, keepdims=True) + EPS)
    return (y * g.astype(jnp.float32)).astype(x.dtype)


def split_in_proj(proj):
    idx, acc = [], 0
    for s in IN_SIZES[:-1]:
        acc += s
        idx.append(acc)
    return jnp.split(proj, idx, axis=-1)


def t5_bucket(rel):
    nb = REL_BUCKETS // 2
    max_exact = nb // 2
    ret = (rel > 0).astype(jnp.int32) * nb
    n = jnp.abs(rel)
    nf = jnp.maximum(n, 1).astype(jnp.float32)
    large = max_exact + (jnp.log(nf / max_exact) / math.log(REL_MAX_DIST / max_exact) * (nb - max_exact)).astype(jnp.int32)
    large = jnp.minimum(large, nb - 1)
    return ret + jnp.where(n < max_exact, n, large)


def diff_attention(q, k, v, lam, rel_bias):
    B, S, H, _, d = q.shape
    nb = S // Q_BLOCK
    scale = d ** -0.5
    qb = q.reshape(B, nb, Q_BLOCK, H, 2, d).transpose(1, 0, 2, 3, 4, 5)
    starts = jnp.arange(nb, dtype=jnp.int32) * Q_BLOCK
    kpos = jnp.arange(S, dtype=jnp.int32)
    qoff = jnp.arange(Q_BLOCK, dtype=jnp.int32)

    def one_block(args):
        qblk, s0 = args
        rel = kpos[None, :] - (s0 + qoff)[:, None]
        bias = jnp.take(rel_bias, t5_bucket(rel), axis=0)
        bias = bias.transpose(2, 0, 1).astype(jnp.float32)
        logits = jnp.einsum('bqhcd,bkhcd->bhcqk', qblk, k).astype(jnp.float32) * scale + bias[None, :, None]
        a = jax.nn.softmax(logits, axis=-1)
        w = a[:, :, 0] - lam * a[:, :, 1]
        return jnp.einsum('bhqk,bkhv->bqhv', w, v.astype(jnp.float32))

    o = lax.map(one_block, (qb, starts))
    return o.transpose(1, 0, 2, 3, 4).reshape(B, S, H, v.shape[-1])


def gla_chunked(q, k, v, log_a, strict):
    B, S, H, dk = q.shape
    dv = v.shape[-1]
    C = GLA_CHUNK
    nc = S // C
    f32 = jnp.float32
    q = q.astype(f32).reshape(B, nc, C, H, dk)
    k = k.astype(f32).reshape(B, nc, C, H, dk)
    v = v.astype(f32).reshape(B, nc, C, H, dv)
    b = jnp.cumsum(log_a.astype(f32).reshape(B, nc, C, H, dk), axis=2)
    b_last = b[:, :, -1:]
    qe = q * jnp.exp(b)
    ke = k * jnp.exp(-b)
    kd = k * jnp.exp(b_last - b)
    mask = jnp.tril(jnp.ones((C, C), dtype=bool), k=-1 if strict else 0)
    att = jnp.einsum('bnthk,bnshk->bnhts', qe, ke)
    att = jnp.where(mask, att, 0.0)
    o = jnp.einsum('bnhts,bnshv->bnthv', att, v)
    kv = jnp.einsum('bnshk,bnshv->bnhkv', kd, v)
    decay = jnp.exp(b_last[:, :, 0])

    def step(state, inp):
        dec, kvc = inp
        return dec[..., None] * state + kvc, state

    _, states = lax.scan(step, jnp.zeros((B, H, dk, dv), f32),
                         (decay.transpose(1, 0, 2, 3), kv.transpose(1, 0, 2, 3, 4)))
    states = states.transpose(1, 0, 2, 3, 4)
    o = o + jnp.einsum('bnthk,bnhkv->bnthv', qe, states)
    return o.reshape(B, S, H, dv)


def peer_ffn(h, w_q, sub_keys, u_tab, v_tab):
    B, S, D = h.shape
    tokens = h.reshape(-1, PEER_TOKEN_BLOCK, D)

    def one_block(hb):
        q = jnp.einsum('td,de->te', hb, w_q).reshape(-1, PEER_HEADS, 2, PEER_HALF)
        s = jnp.einsum('thcd,hcnd->thcn', q, sub_keys).astype(jnp.float32)
        v1, i1 = lax.top_k(s[:, :, 0], PEER_TOPK)
        v2, i2 = lax.top_k(s[:, :, 1], PEER_TOPK)
        cand = (v1[..., :, None] + v2[..., None, :]).reshape(-1, PEER_HEADS, PEER_TOPK * PEER_TOPK)
        cidx = (i1[..., :, None] * PEER_N_KEYS + i2[..., None, :]).reshape(-1, PEER_HEADS, PEER_TOPK * PEER_TOPK)
        top, pos = lax.top_k(cand, PEER_TOPK)
        idx = jnp.take_along_axis(cidx, pos, axis=-1)
        g = jax.nn.softmax(top, axis=-1)
        u = jnp.take(u_tab, idx, axis=0)
        act = jax.nn.gelu(jnp.einsum('thkd,td->thk', u, hb).astype(jnp.float32))
        vv = jnp.take(v_tab, idx, axis=0)
        return jnp.einsum('thk,thkd->td', (g * act).astype(vv.dtype), vv)

    out = lax.map(one_block, tokens)
    return out.reshape(B, S, D).astype(h.dtype)


def encoder_layer(x, p_i, layer_idx, rel_bias, norm1_g, w_in, lambda_qk, da_norm_g,
                  gla_alpha_w, gla_alpha_b, gla_norm_g, w_up_a, w_up_b, w_out,
                  norm2_g, peer_w_q, peer_sub_keys, peer_u, peer_v, norm3_g, ple_w, ple_gate_w):
    B, S, D = x.shape
    h = rmsnorm(x, norm1_g)
    proj = jnp.einsum('bsd,de->bse', h, w_in)
    qa, ka, va, qg, kg, vg, rg, lr, gl = split_in_proj(proj)

    lam_init = 0.8 - 0.6 * math.exp(-0.3 * layer_idx)
    lq = lambda_qk.astype(jnp.float32)
    lam = jnp.exp(jnp.sum(lq[0] * lq[1])) - jnp.exp(jnp.sum(lq[2] * lq[3])) + lam_init
    oa = diff_attention(qa.reshape(B, S, DA_HEADS, 2, DA_HEAD_DIM),
                        ka.reshape(B, S, DA_HEADS, 2, DA_HEAD_DIM),
                        va.reshape(B, S, DA_HEADS, DA_V_DIM), lam, rel_bias)
    oa = rmsnorm(oa, da_norm_g) * (1.0 - lam_init)
    ya = jnp.einsum('bse,ed->bsd', oa.reshape(B, S, DA_WIDTH).astype(x.dtype), w_up_a)

    qg = qg.reshape(B, S, GLA_HEADS, GLA_DK) * (GLA_DK ** -0.5)
    kg = kg.reshape(B, S, GLA_HEADS, GLA_DK)
    vg = vg.reshape(B, S, GLA_HEADS, GLA_DV)
    z = jnp.einsum('bsjr,jrk->bsjk', lr.reshape(B, S, 2, GLA_GATE_RANK), gla_alpha_w) + gla_alpha_b
    log_a = (jax.nn.log_sigmoid(z.astype(jnp.float32)) / GLA_GATE_TAU).reshape(B, S, 2, GLA_HEADS, GLA_DK)
    o_fwd = gla_chunked(qg, kg, vg, log_a[:, :, 0], strict=False)
    flip = lambda t: jnp.flip(t, axis=1)
    o_bwd = flip(gla_chunked(flip(qg), flip(kg), flip(vg), flip(log_a[:, :, 1]), strict=True))
    og = rmsnorm(o_fwd + o_bwd, gla_norm_g).reshape(B, S, GLA_WIDTH)
    og = (og * jax.nn.silu(rg.astype(jnp.float32))).astype(x.dtype)
    yb = jnp.einsum('bse,ed->bsd', og, w_up_b)

    gates = jax.nn.sigmoid(gl.reshape(B, S, N_BRANCHES, D).astype(jnp.float32)).astype(x.dtype)
    merged = gates[:, :, 0] * ya + gates[:, :, 1] * yb
    x = x + jnp.einsum('bsd,de->bse', merged, w_out)

    x = x + peer_ffn(rmsnorm(x, norm2_g), peer_w_q, peer_sub_keys, peer_u, peer_v)

    h3 = rmsnorm(x, norm3_g)
    gate = jax.nn.sigmoid(jnp.einsum('bsd,de->bse', h3, ple_gate_w).astype(jnp.float32)).astype(x.dtype)
    x = x + jnp.einsum('bsp,pd->bsd', p_i, ple_w) * gate
    return x


def run_trunk(x, p, rel_bias, norm1_g, w_in, lambda_qk, da_norm_g, gla_alpha_w, gla_alpha_b,
              gla_norm_g, w_up_a, w_up_b, w_out, norm2_g, peer_w_q, peer_sub_keys, peer_u,
              peer_v, norm3_g, ple_w, ple_gate_w, final_norm_g):
    for i in range(DEPTH):
        x = encoder_layer(x, p[i], i, rel_bias, norm1_g[i], w_in[i], lambda_qk[i], da_norm_g[i],
                          gla_alpha_w[i], gla_alpha_b[i], gla_norm_g[i], w_up_a[i], w_up_b[i], w_out[i],
                          norm2_g[i], peer_w_q[i], peer_sub_keys[i], peer_u[i], peer_v[i],
                          norm3_g[i], ple_w[i], ple_gate_w[i])
    return rmsnorm(x, final_norm_g)


def setup_inputs(seed: int = 0) -> dict:
    key = jax.random.key(seed)
    ks = jax.random.split(key, 24)
    f32 = jnp.float32

    def nrm(k, shape, scale):
        return jax.random.normal(k, shape, f32) * scale

    def gain(k, shape):
        return 1.0 + 0.05 * jax.random.normal(k, shape, f32)

    return {
        "x_prompt": nrm(ks[0], (BATCH, SEQ, D_MODEL), 1.0),
        "x_sample": nrm(ks[1], (DEC_BATCH, DEC_SEQ, D_MODEL), 1.0),
        "p_prompt": nrm(ks[2], (DEPTH, BATCH, SEQ, PLE_DIM), 1.0),
        "p_sample": nrm(ks[3], (DEPTH, DEC_BATCH, DEC_SEQ, PLE_DIM), 1.0),
        "rel_bias": nrm(ks[4], (REL_BUCKETS, DA_HEADS), 0.5),
        "norm1_g": gain(ks[5], (DEPTH, D_MODEL)),
        "w_in": nrm(ks[6], (DEPTH, D_MODEL, IN_DIM), D_MODEL ** -0.5),
        "lambda_qk": nrm(ks[7], (DEPTH, 4, DA_HEAD_DIM), 0.1),
        "da_norm_g": gain(ks[8], (DEPTH, DA_V_DIM)),
        "gla_alpha_w": nrm(ks[9], (DEPTH, 2, GLA_GATE_RANK, GLA_QK), GLA_GATE_RANK ** -0.5),
        "gla_alpha_b": nrm(ks[10], (DEPTH, 2, GLA_QK), 0.1),
        "gla_norm_g": gain(ks[11], (DEPTH, GLA_DV)),
        "w_up_a": nrm(ks[12], (DEPTH, DA_WIDTH, D_MODEL), DA_WIDTH ** -0.5),
        "w_up_b": nrm(ks[13], (DEPTH, GLA_WIDTH, D_MODEL), GLA_WIDTH ** -0.5),
        "w_out": nrm(ks[14], (DEPTH, D_MODEL, D_MODEL), D_MODEL ** -0.5),
        "norm2_g": gain(ks[15], (DEPTH, D_MODEL)),
        "peer_w_q": nrm(ks[16], (DEPTH, D_MODEL, PEER_HEADS * PEER_QUERY_DIM), D_MODEL ** -0.5),
        "peer_sub_keys": nrm(ks[17], (DEPTH, PEER_HEADS, 2, PEER_N_KEYS, PEER_HALF), PEER_HALF ** -0.5),
        "peer_u": nrm(ks[18], (DEPTH, PEER_N_EXPERTS, D_MODEL), D_MODEL ** -0.5),
        "peer_v": nrm(ks[19], (DEPTH, PEER_N_EXPERTS, D_MODEL), (PEER_HEADS * PEER_TOPK) ** -0.5),
        "norm3_g": gain(ks[20], (DEPTH, D_MODEL)),
        "ple_w": nrm(ks[21], (DEPTH, PLE_DIM, D_MODEL), PLE_DIM ** -0.5),
        "ple_gate_w": nrm(ks[22], (DEPTH, D_MODEL, D_MODEL), D_MODEL ** -0.5),
        "final_norm_g": gain(ks[23], (D_MODEL,)),
    }


def reference(x_prompt, x_sample, p_prompt, p_sample, rel_bias, norm1_g, w_in, lambda_qk, da_norm_g,
              gla_alpha_w, gla_alpha_b, gla_norm_g, w_up_a, w_up_b, w_out, norm2_g, peer_w_q,
              peer_sub_keys, peer_u, peer_v, norm3_g, ple_w, ple_gate_w, final_norm_g):
    y_prompt = run_trunk(x_prompt, p_prompt, rel_bias, norm1_g, w_in, lambda_qk, da_norm_g, gla_alpha_w,
                         gla_alpha_b, gla_norm_g, w_up_a, w_up_b, w_out, norm2_g, peer_w_q, peer_sub_keys,
                         peer_u, peer_v, norm3_g, ple_w, ple_gate_w, final_norm_g)
    y_sample = run_trunk(x_sample, p_sample, rel_bias, norm1_g, w_in, lambda_qk, da_norm_g, gla_alpha_w,
                         gla_alpha_b, gla_norm_g, w_up_a, w_up_b, w_out, norm2_g, peer_w_q, peer_sub_keys,
                         peer_u, peer_v, norm3_g, ple_w, ple_gate_w, final_norm_g)
    return (y_prompt, y_sample)
```

```python
import functools
import math

import jax
import jax.numpy as jnp
from jax import lax
from jax.experimental import pallas as pl
from jax.experimental.pallas import tpu as pltpu

F32 = jnp.float32
BF16 = jnp.bfloat16

D_MODEL = 1024
DA_HEADS = 4
DA_HEAD_DIM = 64
DA_V_DIM = 128
GLA_HEADS = 4
GLA_DK = 64
GLA_DV = 128
GLA_GATE_RANK = 16
GLA_GATE_TAU = 16.0
GLA_CHUNK = 64
REL_BUCKETS = 32
REL_MAX_DIST = 128
PEER_HEADS = 8
PEER_N_KEYS = 128
PEER_TOPK = 16
PLE_DIM = 256
EPS = 1e-6

LANES = 128
V7X_VMEM_BYTES = 64 * 1024 * 1024
VMEM_LIMIT = V7X_VMEM_BYTES - 8 * 1024 * 1024

_C_QA, _C_KA, _C_VA, _C_QG, _C_KG, _C_VG, _C_RG, _C_GL, _C_LR, _C_END = (
    0, 512, 1024, 1536, 2048, 2560, 3072, 3584, 5632, 5760)

ATT_TQ = 256
PEER_TT = 512
PEER_EC = 1024
PEER_LW = 256


def _nt(a, b):
    return lax.dot_general(a, b, (((1,), (1,)), ((), ())), preferred_element_type=F32)


def _tn(a, b):
    return lax.dot_general(a, b, (((0,), (0,)), ((), ())), preferred_element_type=F32)


def _mm(a, b):
    return jnp.dot(a, b, preferred_element_type=F32)


def _rms(x, g):
    return x * lax.rsqrt(jnp.mean(x * x, axis=-1, keepdims=True) + EPS) * g


def _params(*sem):
    return pltpu.CompilerParams(dimension_semantics=sem, vmem_limit_bytes=VMEM_LIMIT)


def _const_spec(shape):
    nd = len(shape)
    return pl.BlockSpec(shape, lambda *_: (0,) * nd)


def _in_proj_kernel(x_ref, g_ref, w_ref, aw_ref, ab_ref,
                    qa_ref, ka_ref, va_ref, qg_ref, kg_ref, vg_ref, rg_ref, gate_ref, la_ref):
    h = _rms(x_ref[...], g_ref[...]).astype(BF16)

    def proj(lo, hi):
        return _mm(h, w_ref[:, lo:hi])

    qa_ref[...] = proj(_C_QA, _C_KA).astype(BF16)
    ka_ref[...] = proj(_C_KA, _C_VA).astype(BF16)
    va_ref[...] = proj(_C_VA, _C_QG).astype(BF16)
    qg_ref[...] = proj(_C_QG, _C_KG)
    kg_ref[...] = proj(_C_KG, _C_VG)
    vg_ref[...] = proj(_C_VG, _C_RG)
    rg = proj(_C_RG, _C_GL)
    rg_ref[...] = rg * jax.nn.sigmoid(rg)
    gate_ref[...] = jax.nn.sigmoid(proj(_C_GL, _C_LR))
    lr = proj(_C_LR, _C_END).astype(BF16)
    z = _mm(lr, aw_ref[...]) + ab_ref[...]
    la_ref[...] = (jnp.minimum(z, 0.0) - jnp.log1p(jnp.exp(-jnp.abs(z)))) * (1.0 / GLA_GATE_TAU)


def _in_proj(x, g, w, aw, ab, tm=256):
    t = x.shape[0]
    row = lambda c: pl.BlockSpec((tm, c), lambda i: (i, 0))
    outs = [(512, BF16)] * 3 + [(512, F32)] * 4 + [(2048, F32), (512, F32)]
    return pl.pallas_call(
        _in_proj_kernel,
        grid=(t // tm,),
        in_specs=[row(D_MODEL), _const_spec(g.shape), _const_spec(w.shape),
                  _const_spec(aw.shape), _const_spec(ab.shape)],
        out_specs=[row(c) for c, _ in outs],
        out_shape=[jax.ShapeDtypeStruct((t, c), dt) for c, dt in outs],
        compiler_params=_params("parallel"),
        name="in_proj",
    )(x, g, w, aw, ab)


def _attn_kernel(q_ref, k_ref, v_ref, bias_ref, lq_ref, g_ref, o_ref, *, tq, nk, lam_init):
    i = pl.program_id(2)
    q = q_ref[...]
    lane = lax.broadcasted_iota(jnp.int32, q.shape, 1)
    zero = jnp.zeros_like(q)
    qb = jnp.concatenate([jnp.where(lane < DA_HEAD_DIM, q, zero),
                          jnp.where(lane >= DA_HEAD_DIM, q, zero)], axis=0)

    def body(j, carry):
        m, l, acc = carry
        off = pl.multiple_of(j * tq, tq)
        kc = k_ref[pl.ds(off, tq), :]
        vc = v_ref[pl.ds(off, tq), :]
        s = _nt(qb, kc)
        b = bias_ref[jnp.clip(j - i, -2, 2) + 2]
        s = s + jnp.concatenate([b, b], axis=0)
        m_new = jnp.maximum(m, jnp.max(s, axis=-1, keepdims=True))
        p = jnp.exp(s - m_new)
        alpha = jnp.exp(m - m_new)
        l = alpha * l + jnp.sum(p, axis=-1, keepdims=True)
        acc = alpha * acc + _mm(p.astype(BF16), vc)
        return m_new, l, acc

    m0 = jnp.full((2 * tq, 1), -jnp.inf, F32)
    l0 = jnp.zeros((2 * tq, 1), F32)
    a0 = jnp.zeros((2 * tq, DA_V_DIM), F32)
    _, l, acc = lax.fori_loop(0, nk, body, (m0, l0, a0))

    lq = lq_ref[...]
    lam = (jnp.exp(jnp.sum(lq[0:1] * lq[1:2], axis=-1, keepdims=True))
           - jnp.exp(jnp.sum(lq[2:3] * lq[3:4], axis=-1, keepdims=True)) + lam_init)
    o = acc[:tq] / l[:tq] - lam * (acc[tq:] / l[tq:])
    o_ref[...] = (_rms(o, g_ref[...]) * (1.0 - lam_init)).astype(BF16)


def _diff_attn(qa, ka, va, bias_tab, lq, g, batch, seq, lam_init):
    tq = ATT_TQ
    nq = seq // tq
    return pl.pallas_call(
        functools.partial(_attn_kernel, tq=tq, nk=nq, lam_init=lam_init),
        grid=(batch, DA_HEADS, nq),
        in_specs=[
            pl.BlockSpec((tq, LANES), lambda b, h, i: (b * nq + i, h)),
            pl.BlockSpec((seq, LANES), lambda b, h, i: (b, h)),
            pl.BlockSpec((seq, LANES), lambda b, h, i: (b, h)),
            pl.BlockSpec((None, 5, tq, tq), lambda b, h, i: (h, 0, 0, 0)),
            _const_spec(lq.shape),
            _const_spec(g.shape),
        ],
        out_specs=pl.BlockSpec((tq, LANES), lambda b, h, i: (b * nq + i, h)),
        out_shape=jax.ShapeDtypeStruct(qa.shape, BF16),
        compiler_params=_params("parallel", "parallel", "arbitrary"),
        name="diff_attn",
    )(qa, ka, va, bias_tab, lq, g)


def _split3(x):
    hi = x.astype(BF16)
    r = x - hi.astype(F32)
    mid = r.astype(BF16)
    lo = (r - mid.astype(F32)).astype(BF16)
    return hi, mid, lo


def _gla_kernel(q_ref, k_ref, v_ref, la_ref, rg_ref, g_ref, o_ref,
                qe_s, kv_s, dec_s, oacc_s, *, nc, rows):
    c = GLA_CHUNK
    r_i = lax.broadcasted_iota(jnp.int32, (c, c), 0)
    c_i = lax.broadcasted_iota(jnp.int32, (c, c), 1)
    causal = c_i <= r_i
    pre = causal.astype(F32).astype(BF16)
    suf = (c_i >= r_i).astype(F32).astype(BF16)
    fwd_lane = lax.broadcasted_iota(jnp.int32, (c, LANES), 1) < GLA_DK

    def chunk(n, _):
        off = pl.multiple_of(n * c, c)
        la = la_ref[pl.ds(off, c), :]
        b = jnp.zeros((c, LANES), F32)
        for part in _split3(jnp.where(fwd_lane, la, 0.0)):
            b = b + _mm(pre, part)
        for part in _split3(jnp.where(fwd_lane, 0.0, la)):
            b = b + _mm(suf, part)
        b_last = jnp.where(fwd_lane[0:1], b[c - 1:c], b[0:1])
        q = q_ref[pl.ds(off, c), :]
        k = k_ref[pl.ds(off, c), :]
        vb = v_ref[pl.ds(off, c), :].astype(BF16)
        qe = q * jnp.exp(b)
        qe_s[pl.ds(off, c), :] = qe
        qeb = qe.astype(BF16)
        keb = (k * jnp.exp(-b)).astype(BF16)
        kdb = (k * jnp.exp(b_last - b)).astype(BF16)
        zero = jnp.zeros_like(qeb)
        att_f = _nt(jnp.where(fwd_lane, qeb, zero), keb)
        att_b = _nt(jnp.where(fwd_lane, zero, qeb), keb)
        att = jnp.where(causal, att_f, att_b)
        oacc_s[pl.ds(off, c), :] = _mm(att.astype(BF16), vb)
        kv_s[n] = _tn(vb, kdb)
        dec_s[n] = jnp.broadcast_to(jnp.exp(b_last), (8, LANES))
        return 0

    lax.fori_loop(0, nc, chunk, 0)

    def sweep(forward):
        def step(t, state):
            n = t if forward else nc - 1 - t
            off = pl.multiple_of(n * c, c)
            qe = qe_s[pl.ds(off, c), :]
            qm = jnp.where(fwd_lane, qe, 0.0) if forward else jnp.where(fwd_lane, 0.0, qe)
            oacc_s[pl.ds(off, c), :] += _nt(qm.astype(BF16), state.astype(BF16))
            return state * dec_s[n][0:1] + kv_s[n]
        lax.fori_loop(0, nc, step, jnp.zeros((GLA_DV, LANES), F32))

    sweep(True)
    sweep(False)

    def fin(r, _):
        off = pl.multiple_of(r * rows, rows)
        o = oacc_s[pl.ds(off, rows), :]
        o_ref[pl.ds(off, rows), :] = (_rms(o, g_ref[...]) * rg_ref[pl.ds(off, rows), :]).astype(BF16)
        return 0

    lax.fori_loop(0, (nc * c) // rows, fin, 0)


def _gla(qg, kg, vg, la, rgs, g, batch, seq):
    nc = seq // GLA_CHUNK
    blk = pl.BlockSpec((seq, LANES), lambda b, h: (b, h))
    return pl.pallas_call(
        functools.partial(_gla_kernel, nc=nc, rows=256),
        grid=(batch, GLA_HEADS),
        in_specs=[blk, blk, blk, blk, blk, _const_spec(g.shape)],
        out_specs=blk,
        out_shape=jax.ShapeDtypeStruct(vg.shape, BF16),
        scratch_shapes=[
            pltpu.VMEM((seq, LANES), F32),
            pltpu.VMEM((nc, GLA_DV, LANES), F32),
            pltpu.VMEM((nc, 8, LANES), F32),
            pltpu.VMEM((seq, LANES), F32),
        ],
        compiler_params=_params("parallel", "parallel"),
        name="gla",
    )(qg, kg, vg, la, rgs, g)


def _mix_kernel(oa_ref, og_ref, gate_ref, x_ref, wa_ref, wb_ref, wo_ref, g2_ref, wq_ref, sk_ref,
                x1_ref, h2_ref, st_ref):
    ya = _mm(oa_ref[...], wa_ref[...])
    yb = _mm(og_ref[...], wb_ref[...])
    merged = gate_ref[:, :D_MODEL] * ya + gate_ref[:, D_MODEL:] * yb
    x1 = x_ref[...] + _mm(merged.astype(BF16), wo_ref[...])
    x1_ref[...] = x1
    h2 = _rms(x1, g2_ref[...]).astype(BF16)
    h2_ref[...] = h2
    q = _mm(h2, wq_ref[...]).astype(BF16)
    for hc in range(2 * PEER_HEADS):
        st_ref[hc] = _nt(sk_ref[hc], q[:, hc * LANES:(hc + 1) * LANES])


def _mix(oa, og, gates, x, wa, wb, wo, g2, wq, sk, tm=256):
    t = x.shape[0]
    row = lambda c: pl.BlockSpec((tm, c), lambda i: (i, 0))
    return pl.pallas_call(
        _mix_kernel,
        grid=(t // tm,),
        in_specs=[row(512), row(512), row(2048), row(D_MODEL)]
        + [_const_spec(a.shape) for a in (wa, wb, wo, g2, wq, sk)],
        out_specs=[row(D_MODEL), row(D_MODEL),
                   pl.BlockSpec((2 * PEER_HEADS, PEER_N_KEYS, tm), lambda i: (0, 0, i))],
        out_shape=[jax.ShapeDtypeStruct((t, D_MODEL), F32),
                   jax.ShapeDtypeStruct((t, D_MODEL), BF16),
                   jax.ShapeDtypeStruct((2 * PEER_HEADS, PEER_N_KEYS, t), F32)],
        compiler_params=_params("parallel"),
        name="mix",
    )(oa, og, gates, x, wa, wb, wo, g2, wq, sk)


def _extract_top(cur, count):
    rows = lax.broadcasted_iota(jnp.int32, cur.shape, 0)
    out = []
    for _ in range(count):
        m = jnp.max(cur, axis=0, keepdims=True)
        out.append(m)
        first = jnp.min(jnp.where(cur == m, rows, cur.shape[0]), axis=0, keepdims=True)
        cur = jnp.where(rows == first, -jnp.inf, cur)
    return out


_PAIRS = [(a, b) for a in range(PEER_TOPK) for b in range(PEER_TOPK) if (a + 1) * (b + 1) <= PEER_TOPK]
_CAND_ROWS = -(-len(_PAIRS) // 8) * 8


def _topk_kernel(st_ref, f1_ref, e2_ref, thr_ref, cand_s):
    lanes = st_ref.shape[-1]
    for h in range(PEER_HEADS):
        s1 = st_ref[2 * h]
        s2 = st_ref[2 * h + 1]
        v1 = _extract_top(s1, PEER_TOPK)
        v2 = _extract_top(s2, PEER_TOPK)
        cand_s[...] = jnp.full((_CAND_ROWS, lanes), -jnp.inf, F32)
        for r, (a, b) in enumerate(_PAIRS):
            cand_s[r:r + 1, :] = v1[a] + v2[b]
        top = _extract_top(cand_s[...], PEER_TOPK)
        z = jnp.zeros_like(top[0])
        for tk in top:
            z = z + jnp.exp(tk - top[0])
        thr_ref[h:h + 1, :] = top[PEER_TOPK - 1]
        f1_ref[h] = jnp.exp(s1 - v1[0]) / z
        e2_ref[h] = jnp.exp(s2 - v2[0])


def _peer_topk(st, tt=256):
    t = st.shape[-1]
    return pl.pallas_call(
        _topk_kernel,
        grid=(t // tt,),
        in_specs=[pl.BlockSpec((2 * PEER_HEADS, PEER_N_KEYS, tt), lambda i: (0, 0, i))],
        out_specs=[pl.BlockSpec((PEER_HEADS, PEER_N_KEYS, tt), lambda i: (0, 0, i)),
                   pl.BlockSpec((PEER_HEADS, PEER_N_KEYS, tt), lambda i: (0, 0, i)),
                   pl.BlockSpec((PEER_HEADS, tt), lambda i: (0, i))],
        out_shape=[jax.ShapeDtypeStruct((PEER_HEADS, PEER_N_KEYS, t), F32),
                   jax.ShapeDtypeStruct((PEER_HEADS, PEER_N_KEYS, t), F32),
                   jax.ShapeDtypeStruct((PEER_HEADS, t), F32)],
        scratch_shapes=[pltpu.VMEM((_CAND_ROWS, tt), F32)],
        compiler_params=_params("parallel"),
        name="peer_topk",
    )(st)


_GELU_C = math.sqrt(2.0 / math.pi)


def _gelu_tanh(x):
    return 0.5 * x * (1.0 + jnp.tanh(_GELU_C * (x + 0.044715 * (x * x * x))))


def _peer_kernel(h2_ref, u_ref, vt_ref, st_ref, f1_ref, e2_ref, thr_ref, x1_ref, x2_ref,
                 acc_s, ut_s, wt_s, *, rows_per_step, lw):
    j = pl.program_id(1)
    tt = h2_ref.shape[0]

    @pl.when(j == 0)
    def _():
        acc_s[...] = jnp.zeros_like(acc_s)

    ut_s[...] = _nt(u_ref[...], h2_ref[...])

    for c0 in range(0, tt, lw):
        cs = slice(c0, c0 + lw)

        def row_block(r, _):
            i1 = j * rows_per_step + r
            gate = jnp.zeros((PEER_N_KEYS, lw), F32)
            for h in range(PEER_HEADS):
                s1 = st_ref[2 * h, pl.ds(i1, 1), cs]
                f1 = f1_ref[h, pl.ds(i1, 1), cs]
                sel = (s1 + st_ref[2 * h + 1, :, cs]) >= thr_ref[h:h + 1, cs]
                gate = gate + jnp.where(sel, e2_ref[h, :, cs] * f1, 0.0)
            off = pl.multiple_of(r * PEER_N_KEYS, PEER_N_KEYS)
            act = _gelu_tanh(ut_s[pl.ds(off, PEER_N_KEYS), cs])
            wt_s[pl.ds(off, PEER_N_KEYS), cs] = (gate * act).astype(BF16)
            return 0

        lax.fori_loop(0, rows_per_step, row_block, 0)

    acc_s[...] += _mm(vt_ref[...], wt_s[...])

    @pl.when(j == pl.num_programs(1) - 1)
    def _():
        x2_ref[...] = x1_ref[...] + acc_s[...].T


def _peer_dense(h2, u, vt, st, f1, e2, thr, x1):
    t = h2.shape[0]
    tt, ec = PEER_TT, PEER_EC
    n_exp = u.shape[0]
    tab = lambda n: pl.BlockSpec((n, PEER_N_KEYS, tt), lambda i, j: (0, 0, i))
    return pl.pallas_call(
        functools.partial(_peer_kernel, rows_per_step=ec // PEER_N_KEYS, lw=PEER_LW),
        grid=(t // tt, n_exp // ec),
        in_specs=[
            pl.BlockSpec((tt, D_MODEL), lambda i, j: (i, 0)),
            pl.BlockSpec((ec, D_MODEL), lambda i, j: (j, 0)),
            pl.BlockSpec((D_MODEL, ec), lambda i, j: (0, j)),
            tab(2 * PEER_HEADS), tab(PEER_HEADS), tab(PEER_HEADS),
            pl.BlockSpec((PEER_HEADS, tt), lambda i, j: (0, i)),
            pl.BlockSpec((tt, D_MODEL), lambda i, j: (i, 0)),
        ],
        out_specs=pl.BlockSpec((tt, D_MODEL), lambda i, j: (i, 0)),
        out_shape=jax.ShapeDtypeStruct((t, D_MODEL), F32),
        scratch_shapes=[pltpu.VMEM((D_MODEL, tt), F32),
                        pltpu.VMEM((ec, tt), F32),
                        pltpu.VMEM((ec, tt), BF16)],
        compiler_params=_params("parallel", "arbitrary"),
        name="peer_dense",
    )(h2, u, vt, st, f1, e2, thr, x1)


def _ple_kernel(x_ref, p_ref, g3_ref, wg_ref, wp_ref, gf_ref, y_ref, *, final_norm):
    x = x_ref[...]
    h3 = _rms(x, g3_ref[...]).astype(BF16)
    gate = jax.nn.sigmoid(_mm(h3, wg_ref[...]))
    x3 = x + _mm(p_ref[...].astype(BF16), wp_ref[...]) * gate
    y_ref[...] = _rms(x3, gf_ref[...]) if final_norm else x3


def _ple(x, p, g3, wg, wp, gf, final_norm, tm=512):
    t = x.shape[0]
    row = lambda c: pl.BlockSpec((tm, c), lambda i: (i, 0))
    return pl.pallas_call(
        functools.partial(_ple_kernel, final_norm=final_norm),
        grid=(t // tm,),
        in_specs=[row(D_MODEL), row(PLE_DIM)] + [_const_spec(a.shape) for a in (g3, wg, wp, gf)],
        out_specs=row(D_MODEL),
        out_shape=jax.ShapeDtypeStruct((t, D_MODEL), F32),
        compiler_params=_params("parallel"),
        name="ple",
    )(x, p, g3, wg, wp, gf)


def _t5_bucket(rel):
    nb = REL_BUCKETS // 2
    max_exact = nb // 2
    ret = (rel > 0).astype(jnp.int32) * nb
    n = jnp.abs(rel)
    nf = jnp.maximum(n, 1).astype(jnp.float32)
    large = max_exact + (jnp.log(nf / max_exact) / math.log(REL_MAX_DIST / max_exact)
                         * (nb - max_exact)).astype(jnp.int32)
    large = jnp.minimum(large, nb - 1)
    return ret + jnp.where(n < max_exact, n, large)


def _bias_table(rel_bias, tq):
    e = jnp.arange(5, dtype=jnp.int32)[:, None, None]
    r = jnp.arange(tq, dtype=jnp.int32)[None, :, None]
    c = jnp.arange(tq, dtype=jnp.int32)[None, None, :]
    bucket = _t5_bucket(c - r + (e - 2) * tq)
    return jnp.take(rel_bias.astype(F32), bucket, axis=0).transpose(3, 0, 1, 2)


def _pack_in_proj(w_in, alpha_w, alpha_b):
    qa, ka, va, qg, kg, vg, rg, lr, gl = jnp.split(
        w_in, [512, 1024, 1536, 1792, 2048, 2560, 3072, 3104], axis=1)
    dup = lambda w: jnp.concatenate([w.reshape(-1, GLA_HEADS, 1, GLA_DK)] * 2, axis=2).reshape(-1, 512)
    lr_pad = jnp.pad(lr, ((0, 0), (0, LANES - lr.shape[1])))
    w = jnp.concatenate([qa * DA_HEAD_DIM ** -0.5, ka, va, dup(qg) * GLA_DK ** -0.5, dup(kg),
                         vg, rg, gl, lr_pad], axis=1).astype(BF16)
    aw = jnp.zeros((LANES, GLA_HEADS, 2, GLA_DK), F32)
    for d in range(2):
        aw = aw.at[d * GLA_GATE_RANK:(d + 1) * GLA_GATE_RANK, :, d, :].set(
            alpha_w[d].reshape(GLA_GATE_RANK, GLA_HEADS, GLA_DK))
    ab = alpha_b.reshape(2, GLA_HEADS, GLA_DK).transpose(1, 0, 2).reshape(1, 512)
    return w, aw.reshape(LANES, 512).astype(BF16), ab.astype(F32)


def _trunk(x, p, rel_bias, norm1_g, w_in, lambda_qk, da_norm_g, gla_alpha_w, gla_alpha_b, gla_norm_g,
           w_up_a, w_up_b, w_out, norm2_g, peer_w_q, peer_sub_keys, peer_u, peer_v, norm3_g,
           ple_w, ple_gate_w, final_norm_g, prepared):
    batch, seq, _ = x.shape
    depth = w_in.shape[0]
    xf = x.reshape(batch * seq, D_MODEL)
    row = lambda v: v.reshape(1, -1).astype(F32)
    for i in range(depth):
        w1, aw, ab, bias_tab, u_bf, vt_bf, sk = prepared[i]
        lam_init = 0.8 - 0.6 * math.exp(-0.3 * i)
        qa, ka, va, qg, kg, vg, rgs, gates, la = _in_proj(xf, row(norm1_g[i]), w1, aw, ab)
        oa = _diff_attn(qa, ka, va, bias_tab, lambda_qk[i].astype(F32), row(da_norm_g[i]),
                        batch, seq, lam_init)
        og = _gla(qg, kg, vg, la, rgs, row(gla_norm_g[i]), batch, seq)
        x1, h2, st = _mix(oa, og, gates, xf, w_up_a[i].astype(BF16), w_up_b[i].astype(BF16),
                          w_out[i].astype(BF16), row(norm2_g[i]), peer_w_q[i].astype(BF16), sk)
        f1, e2, thr = _peer_topk(st)
        x2 = _peer_dense(h2, u_bf, vt_bf, st, f1, e2, thr, x1)
        xf = _ple(x2, p[i].reshape(batch * seq, PLE_DIM), row(norm3_g[i]), ple_gate_w[i].astype(BF16),
                  ple_w[i].astype(BF16), row(final_norm_g), final_norm=(i == depth - 1))
    return xf.reshape(batch, seq, D_MODEL)


def kernel(x_prompt, x_sample, p_prompt, p_sample, rel_bias, norm1_g, w_in, lambda_qk, da_norm_g,
           gla_alpha_w, gla_alpha_b, gla_norm_g, w_up_a, w_up_b, w_out, norm2_g, peer_w_q,
           peer_sub_keys, peer_u, peer_v, norm3_g, ple_w, ple_gate_w, final_norm_g):
    prepared = []
    for i in range(w_in.shape[0]):
        w1, aw, ab = _pack_in_proj(w_in[i], gla_alpha_w[i], gla_alpha_b[i])
        prepared.append((w1, aw, ab, _bias_table(rel_bias, ATT_TQ), peer_u[i].astype(BF16),
                         peer_v[i].T.astype(BF16),
                         peer_sub_keys[i].reshape(2 * PEER_HEADS, PEER_N_KEYS, -1).astype(BF16)))
    shared = (rel_bias, norm1_g, w_in, lambda_qk, da_norm_g, gla_alpha_w, gla_alpha_b, gla_norm_g,
              w_up_a, w_up_b, w_out, norm2_g, peer_w_q, peer_sub_keys, peer_u, peer_v, norm3_g,
              ple_w, ple_gate_w, final_norm_g)
    y_prompt = _trunk(x_prompt, p_prompt, *shared, prepared)
    y_sample = _trunk(x_sample, p_sample, *shared, prepared)
    return (y_prompt, y_sample)
```

```python
import functools
import math

import jax
import jax.numpy as jnp
from jax import lax
from jax.experimental import pallas as pl
from jax.experimental.pallas import tpu as pltpu

F32 = jnp.float32
BF16 = jnp.bfloat16

D_MODEL = 1024
DA_HEADS = 4
DA_HEAD_DIM = 64
DA_V_DIM = 128
GLA_HEADS = 4
GLA_DK = 64
GLA_DV = 128
GLA_GATE_RANK = 16
GLA_GATE_TAU = 16.0
GLA_CHUNK = 64
REL_BUCKETS = 32
REL_MAX_DIST = 128
PEER_HEADS = 8
PEER_N_KEYS = 128
PEER_TOPK = 16
PLE_DIM = 256
EPS = 1e-6

LANES = 128
V7X_VMEM_BYTES = 64 * 1024 * 1024
VMEM_LIMIT = V7X_VMEM_BYTES - 8 * 1024 * 1024

_C_QA, _C_KA, _C_VA, _C_QG, _C_KG, _C_VG, _C_RG, _C_GL, _C_LR, _C_END = (
    0, 512, 1024, 1536, 2048, 2560, 3072, 3584, 5632, 5760)

ATT_TQ = 256
PEER_TT = 512
PEER_EC = 2048
PEER_SC = 512
PEER_LW = 128


def _nt(a, b):
    return lax.dot_general(a, b, (((1,), (1,)), ((), ())), preferred_element_type=F32)


def _tn(a, b):
    return lax.dot_general(a, b, (((0,), (0,)), ((), ())), preferred_element_type=F32)


def _mm(a, b):
    return jnp.dot(a, b, preferred_element_type=F32)


def _rms(x, g):
    return x * lax.rsqrt(jnp.mean(x * x, axis=-1, keepdims=True) + EPS) * g


def _params(*sem):
    return pltpu.CompilerParams(dimension_semantics=sem, vmem_limit_bytes=VMEM_LIMIT)


def _const_spec(shape):
    nd = len(shape)
    return pl.BlockSpec(shape, lambda *_: (0,) * nd)


def _in_proj_kernel(x_ref, g_ref, w_ref, aw_ref, ab_ref,
                    qa_ref, ka_ref, va_ref, qg_ref, kg_ref, vg_ref, rg_ref, gate_ref, la_ref):
    h = _rms(x_ref[...], g_ref[...]).astype(BF16)

    def proj(lo, hi):
        return _mm(h, w_ref[:, lo:hi])

    qa_ref[...] = proj(_C_QA, _C_KA).astype(BF16)
    ka_ref[...] = proj(_C_KA, _C_VA).astype(BF16)
    va_ref[...] = proj(_C_VA, _C_QG).astype(BF16)
    qg_ref[...] = proj(_C_QG, _C_KG)
    kg_ref[...] = proj(_C_KG, _C_VG)
    vg_ref[...] = proj(_C_VG, _C_RG)
    rg = proj(_C_RG, _C_GL)
    rg_ref[...] = rg * jax.nn.sigmoid(rg)
    gate_ref[...] = jax.nn.sigmoid(proj(_C_GL, _C_LR))
    lr = proj(_C_LR, _C_END).astype(BF16)
    z = _mm(lr, aw_ref[...]) + ab_ref[...]
    la_ref[...] = (jnp.minimum(z, 0.0) - jnp.log1p(jnp.exp(-jnp.abs(z)))) * (1.0 / GLA_GATE_TAU)


def _in_proj(x, g, w, aw, ab, tm=256):
    t = x.shape[0]
    row = lambda c: pl.BlockSpec((tm, c), lambda i: (i, 0))
    outs = [(512, BF16)] * 3 + [(512, F32)] * 4 + [(2048, F32), (512, F32)]
    return pl.pallas_call(
        _in_proj_kernel,
        grid=(t // tm,),
        in_specs=[row(D_MODEL), _const_spec(g.shape), _const_spec(w.shape),
                  _const_spec(aw.shape), _const_spec(ab.shape)],
        out_specs=[row(c) for c, _ in outs],
        out_shape=[jax.ShapeDtypeStruct((t, c), dt) for c, dt in outs],
        compiler_params=_params("parallel"),
        name="in_proj",
    )(x, g, w, aw, ab)


def _attn_kernel(q_ref, k_ref, v_ref, bias_ref, lq_ref, g_ref, o_ref, *, tq, nk, lam_init):
    i = pl.program_id(2)
    q = q_ref[...]
    lane = lax.broadcasted_iota(jnp.int32, q.shape, 1)
    zero = jnp.zeros_like(q)
    qb = jnp.concatenate([jnp.where(lane < DA_HEAD_DIM, q, zero),
                          jnp.where(lane >= DA_HEAD_DIM, q, zero)], axis=0)

    def body(j, carry):
        m, l, acc = carry
        off = pl.multiple_of(j * tq, tq)
        kc = k_ref[pl.ds(off, tq), :]
        vc = v_ref[pl.ds(off, tq), :]
        s = _nt(qb, kc)
        b = bias_ref[jnp.clip(j - i, -2, 2) + 2]
        s = s + jnp.concatenate([b, b], axis=0)
        m_new = jnp.maximum(m, jnp.max(s, axis=-1, keepdims=True))
        p = jnp.exp(s - m_new)
        alpha = jnp.exp(m - m_new)
        l = alpha * l + jnp.sum(p, axis=-1, keepdims=True)
        acc = alpha * acc + _mm(p.astype(BF16), vc)
        return m_new, l, acc

    m0 = jnp.full((2 * tq, 1), -jnp.inf, F32)
    l0 = jnp.zeros((2 * tq, 1), F32)
    a0 = jnp.zeros((2 * tq, DA_V_DIM), F32)
    _, l, acc = lax.fori_loop(0, nk, body, (m0, l0, a0))

    lq = lq_ref[...]
    lam = (jnp.exp(jnp.sum(lq[0:1] * lq[1:2], axis=-1, keepdims=True))
           - jnp.exp(jnp.sum(lq[2:3] * lq[3:4], axis=-1, keepdims=True)) + lam_init)
    o = acc[:tq] / l[:tq] - lam * (acc[tq:] / l[tq:])
    o_ref[...] = (_rms(o, g_ref[...]) * (1.0 - lam_init)).astype(BF16)


def _diff_attn(qa, ka, va, bias_tab, lq, g, batch, seq, lam_init):
    tq = ATT_TQ
    nq = seq // tq
    return pl.pallas_call(
        functools.partial(_attn_kernel, tq=tq, nk=nq, lam_init=lam_init),
        grid=(batch, DA_HEADS, nq),
        in_specs=[
            pl.BlockSpec((tq, LANES), lambda b, h, i: (b * nq + i, h)),
            pl.BlockSpec((seq, LANES), lambda b, h, i: (b, h)),
            pl.BlockSpec((seq, LANES), lambda b, h, i: (b, h)),
            pl.BlockSpec((None, 5, tq, tq), lambda b, h, i: (h, 0, 0, 0)),
            _const_spec(lq.shape),
            _const_spec(g.shape),
        ],
        out_specs=pl.BlockSpec((tq, LANES), lambda b, h, i: (b * nq + i, h)),
        out_shape=jax.ShapeDtypeStruct(qa.shape, BF16),
        compiler_params=_params("parallel", "parallel", "arbitrary"),
        name="diff_attn",
    )(qa, ka, va, bias_tab, lq, g)


def _split3(x):
    hi = x.astype(BF16)
    r = x - hi.astype(F32)
    mid = r.astype(BF16)
    lo = (r - mid.astype(F32)).astype(BF16)
    return hi, mid, lo


def _gla_kernel(q_ref, k_ref, v_ref, la_ref, rg_ref, g_ref, o_ref,
                qe_s, kv_s, dec_s, oacc_s, *, nc, rows):
    c = GLA_CHUNK
    r_i = lax.broadcasted_iota(jnp.int32, (c, c), 0)
    c_i = lax.broadcasted_iota(jnp.int32, (c, c), 1)
    causal = c_i <= r_i
    pre = causal.astype(F32).astype(BF16)
    suf = (c_i >= r_i).astype(F32).astype(BF16)
    fwd_lane = lax.broadcasted_iota(jnp.int32, (c, LANES), 1) < GLA_DK

    def chunk(n, _):
        off = pl.multiple_of(n * c, c)
        la = la_ref[pl.ds(off, c), :]
        b = jnp.zeros((c, LANES), F32)
        for part in _split3(jnp.where(fwd_lane, la, 0.0)):
            b = b + _mm(pre, part)
        for part in _split3(jnp.where(fwd_lane, 0.0, la)):
            b = b + _mm(suf, part)
        b_last = jnp.where(fwd_lane[0:1], b[c - 1:c], b[0:1])
        q = q_ref[pl.ds(off, c), :]
        k = k_ref[pl.ds(off, c), :]
        vb = v_ref[pl.ds(off, c), :].astype(BF16)
        qe = q * jnp.exp(b)
        qe_s[pl.ds(off, c), :] = qe
        qeb = qe.astype(BF16)
        keb = (k * jnp.exp(-b)).astype(BF16)
        kdb = (k * jnp.exp(b_last - b)).astype(BF16)
        zero = jnp.zeros_like(qeb)
        att_f = _nt(jnp.where(fwd_lane, qeb, zero), keb)
        att_b = _nt(jnp.where(fwd_lane, zero, qeb), keb)
        att = jnp.where(causal, att_f, att_b)
        oacc_s[pl.ds(off, c), :] = _mm(att.astype(BF16), vb)
        kv_s[n] = _tn(vb, kdb)
        dec_s[n] = jnp.broadcast_to(jnp.exp(b_last), (8, LANES))
        return 0

    lax.fori_loop(0, nc, chunk, 0)

    def sweep(forward):
        def step(t, state):
            n = t if forward else nc - 1 - t
            off = pl.multiple_of(n * c, c)
            qe = qe_s[pl.ds(off, c), :]
            qm = jnp.where(fwd_lane, qe, 0.0) if forward else jnp.where(fwd_lane, 0.0, qe)
            oacc_s[pl.ds(off, c), :] += _nt(qm.astype(BF16), state.astype(BF16))
            return state * dec_s[n][0:1] + kv_s[n]
        lax.fori_loop(0, nc, step, jnp.zeros((GLA_DV, LANES), F32))

    sweep(True)
    sweep(False)

    def fin(r, _):
        off = pl.multiple_of(r * rows, rows)
        o = oacc_s[pl.ds(off, rows), :]
        o_ref[pl.ds(off, rows), :] = (_rms(o, g_ref[...]) * rg_ref[pl.ds(off, rows), :]).astype(BF16)
        return 0

    lax.fori_loop(0, (nc * c) // rows, fin, 0)


def _gla(qg, kg, vg, la, rgs, g, batch, seq):
    nc = seq // GLA_CHUNK
    blk = pl.BlockSpec((seq, LANES), lambda b, h: (b, h))
    return pl.pallas_call(
        functools.partial(_gla_kernel, nc=nc, rows=256),
        grid=(batch, GLA_HEADS),
        in_specs=[blk, blk, blk, blk, blk, _const_spec(g.shape)],
        out_specs=blk,
        out_shape=jax.ShapeDtypeStruct(vg.shape, BF16),
        scratch_shapes=[
            pltpu.VMEM((seq, LANES), F32),
            pltpu.VMEM((nc, GLA_DV, LANES), F32),
            pltpu.VMEM((nc, 8, LANES), F32),
            pltpu.VMEM((seq, LANES), F32),
        ],
        compiler_params=_params("parallel", "parallel"),
        name="gla",
    )(qg, kg, vg, la, rgs, g)


def _mix_kernel(oa_ref, og_ref, gate_ref, x_ref, wa_ref, wb_ref, wo_ref, g2_ref, wq_ref, sk_ref,
                x1_ref, h2_ref, st_ref):
    ya = _mm(oa_ref[...], wa_ref[...])
    yb = _mm(og_ref[...], wb_ref[...])
    merged = gate_ref[:, :D_MODEL] * ya + gate_ref[:, D_MODEL:] * yb
    x1 = x_ref[...] + _mm(merged.astype(BF16), wo_ref[...])
    x1_ref[...] = x1
    h2 = _rms(x1, g2_ref[...]).astype(BF16)
    h2_ref[...] = h2
    q = _mm(h2, wq_ref[...]).astype(BF16)
    for hc in range(2 * PEER_HEADS):
        st_ref[hc] = _nt(sk_ref[hc], q[:, hc * LANES:(hc + 1) * LANES])


def _mix(oa, og, gates, x, wa, wb, wo, g2, wq, sk, tm=256):
    t = x.shape[0]
    row = lambda c: pl.BlockSpec((tm, c), lambda i: (i, 0))
    return pl.pallas_call(
        _mix_kernel,
        grid=(t // tm,),
        in_specs=[row(512), row(512), row(2048), row(D_MODEL)]
        + [_const_spec(a.shape) for a in (wa, wb, wo, g2, wq, sk)],
        out_specs=[row(D_MODEL), row(D_MODEL),
                   pl.BlockSpec((2 * PEER_HEADS, PEER_N_KEYS, tm), lambda i: (0, 0, i))],
        out_shape=[jax.ShapeDtypeStruct((t, D_MODEL), F32),
                   jax.ShapeDtypeStruct((t, D_MODEL), BF16),
                   jax.ShapeDtypeStruct((2 * PEER_HEADS, PEER_N_KEYS, t), F32)],
        compiler_params=_params("parallel"),
        name="mix",
    )(oa, og, gates, x, wa, wb, wo, g2, wq, sk)


def _extract_top(cur, count, with_rank=False):
    rows = lax.broadcasted_iota(jnp.int32, cur.shape, 0)
    rank = jnp.full(cur.shape, float(count), F32)
    out = []
    for r in range(count):
        m = jnp.max(cur, axis=0, keepdims=True)
        out.append(m)
        first = jnp.min(jnp.where(cur == m, rows, cur.shape[0]), axis=0, keepdims=True)
        hit = rows == first
        cur = jnp.where(hit, -jnp.inf, cur)
        if with_rank:
            rank = jnp.where(hit, float(r), rank)
    return (out, rank) if with_rank else out


_PAIRS = [(a, b) for a in range(PEER_TOPK) for b in range(PEER_TOPK) if (a + 1) * (b + 1) <= PEER_TOPK]
_CAND_ROWS = -(-len(_PAIRS) // 8) * 8


_HI16 = 0xFFFF0000


def _bf16_hi_bits(x):
    u = lax.bitcast_convert_type(x, jnp.uint32)
    u = u + jnp.uint32(0x7FFF) + ((u >> 16) & jnp.uint32(1))
    return u & jnp.uint32(_HI16)


def _pack_pair(a, b):
    return _bf16_hi_bits(a) | (_bf16_hi_bits(b) >> 16)


def _sum_pair(w):
    return (lax.bitcast_convert_type(w & jnp.uint32(_HI16), F32)
            + lax.bitcast_convert_type(w << 16, F32))


def _topk_kernel(st_ref, r2_ref, e2_ref, n1_ref, f1_ref, cand_s):
    lanes = st_ref.shape[-1]
    pending = None
    for h in range(PEER_HEADS):
        s1 = st_ref[2 * h]
        s2 = st_ref[2 * h + 1]
        v1 = _extract_top(s1, PEER_TOPK)
        v2, rank2 = _extract_top(s2, PEER_TOPK, with_rank=True)
        cand_s[...] = jnp.full((_CAND_ROWS, lanes), -jnp.inf, F32)
        for r, (a, b) in enumerate(_PAIRS):
            cand_s[r:r + 1, :] = v1[a] + v2[b]
        top = _extract_top(cand_s[...], PEER_TOPK)
        thr = top[PEER_TOPK - 1]
        z = jnp.zeros_like(thr)
        for tk in top:
            z = z + jnp.exp(tk - top[0])
        n1 = jnp.zeros_like(s1)
        for b in range(PEER_TOPK):
            n1 = n1 + jnp.where(s1 + v2[b] >= thr, 1.0, 0.0)
        e2 = jnp.exp(s2 - v2[0])
        if h % 2 == 0:
            pending = (rank2, e2)
        else:
            r2_ref[h // 2] = _pack_pair(pending[0], rank2)
            e2_ref[h // 2] = _pack_pair(pending[1], e2)
        n1_ref[h] = n1
        f1_ref[h] = jnp.exp(s1 - v1[0]) / z


def _peer_topk(st, tt=256):
    t = st.shape[-1]
    tab = pl.BlockSpec((PEER_HEADS, PEER_N_KEYS, tt), lambda i: (0, 0, i))
    tab2 = pl.BlockSpec((PEER_HEADS // 2, PEER_N_KEYS, tt), lambda i: (0, 0, i))
    shape = (PEER_HEADS, PEER_N_KEYS, t)
    shape2 = (PEER_HEADS // 2, PEER_N_KEYS, t)
    return pl.pallas_call(
        _topk_kernel,
        grid=(t // tt,),
        in_specs=[pl.BlockSpec((2 * PEER_HEADS, PEER_N_KEYS, tt), lambda i: (0, 0, i))],
        out_specs=[tab2, tab2, tab, tab],
        out_shape=[jax.ShapeDtypeStruct(shape2, jnp.uint32), jax.ShapeDtypeStruct(shape2, jnp.uint32),
                   jax.ShapeDtypeStruct(shape, F32), jax.ShapeDtypeStruct(shape, F32)],
        scratch_shapes=[pltpu.VMEM((_CAND_ROWS, tt), F32)],
        compiler_params=_params("parallel"),
        name="peer_topk",
    )(st)


_GELU_C0 = math.sqrt(2.0 / math.pi)
_GELU_C1 = 0.044715 * _GELU_C0


def _gelu_tanh(x):
    hx = 0.5 * x
    return hx + hx * jnp.tanh(x * (_GELU_C0 + _GELU_C1 * (x * x)))


def _peer_kernel(h2_ref, u_ref, vt_ref, r2_ref, e2_ref, n1_ref, f1_ref, x1_ref, x2_ref,
                 acc_s, ut_s, wt_s, *, ec, sc, lw):
    j = pl.program_id(1)
    tt = h2_ref.shape[0]
    zero = jnp.zeros((), BF16)

    @pl.when(j == 0)
    def _():
        acc_s[...] = jnp.zeros_like(acc_s)

    h2 = h2_ref[...]
    for s0 in range(0, ec, sc):
        ut_s[s0:s0 + sc, :] = _nt(u_ref[s0:s0 + sc, :], h2)
        for r0 in range(s0, s0 + sc, PEER_N_KEYS):
            q = r0 // PEER_N_KEYS
            grp = j * (ec // PEER_N_KEYS // 8) + q // 8
            sub = slice(q % 8, q % 8 + 1)
            for c0 in range(0, tt, lw):
                cs = slice(c0, c0 + lw)
                gate = jnp.zeros((2 * PEER_N_KEYS, lw), BF16)
                for hp in range(PEER_HEADS // 2):
                    row = lambda ref, h: jnp.broadcast_to(ref[h, grp, sub, cs], (PEER_N_KEYS, lw))
                    n1 = pltpu.bitcast(_pack_pair(row(n1_ref, 2 * hp), row(n1_ref, 2 * hp + 1)), BF16)
                    f1 = pltpu.bitcast(_pack_pair(row(f1_ref, 2 * hp), row(f1_ref, 2 * hp + 1)), BF16)
                    r2 = pltpu.bitcast(r2_ref[hp, :, cs], BF16)
                    e2 = pltpu.bitcast(e2_ref[hp, :, cs], BF16)
                    gate = gate + jnp.where(r2 < n1, e2 * f1, zero)
                g = _sum_pair(pltpu.bitcast(gate, jnp.uint32))
                act = _gelu_tanh(ut_s[r0:r0 + PEER_N_KEYS, cs])
                wt_s[r0:r0 + PEER_N_KEYS, cs] = (g * act).astype(BF16)
        acc_s[...] += _mm(vt_ref[:, s0:s0 + sc], wt_s[s0:s0 + sc, :])

    @pl.when(j == pl.num_programs(1) - 1)
    def _():
        x2_ref[...] = x1_ref[...] + acc_s[...].T


def _peer_dense(h2, u, vt, r2, e2, n1, f1, x1):
    t = h2.shape[0]
    tt, ec = PEER_TT, PEER_EC
    n_exp = u.shape[0]
    tab = pl.BlockSpec((PEER_HEADS // 2, PEER_N_KEYS, tt), lambda i, j: (0, 0, i))
    tab1 = pl.BlockSpec((PEER_HEADS, PEER_N_KEYS // 8, 8, tt), lambda i, j: (0, 0, 0, i))
    n1 = n1.reshape(PEER_HEADS, PEER_N_KEYS // 8, 8, t)
    f1 = f1.reshape(PEER_HEADS, PEER_N_KEYS // 8, 8, t)
    return pl.pallas_call(
        functools.partial(_peer_kernel, ec=ec, sc=PEER_SC, lw=PEER_LW),
        grid=(t // tt, n_exp // ec),
        in_specs=[
            pl.BlockSpec((tt, D_MODEL), lambda i, j: (i, 0)),
            pl.BlockSpec((ec, D_MODEL), lambda i, j: (j, 0)),
            pl.BlockSpec((D_MODEL, ec), lambda i, j: (0, j)),
            tab, tab, tab1, tab1,
            pl.BlockSpec((tt, D_MODEL), lambda i, j: (i, 0)),
        ],
        out_specs=pl.BlockSpec((tt, D_MODEL), lambda i, j: (i, 0)),
        out_shape=jax.ShapeDtypeStruct((t, D_MODEL), F32),
        scratch_shapes=[pltpu.VMEM((D_MODEL, tt), F32),
                        pltpu.VMEM((ec, tt), F32),
                        pltpu.VMEM((ec, tt), BF16)],
        compiler_params=_params("parallel", "arbitrary"),
        name="peer_dense",
    )(h2, u, vt, r2, e2, n1, f1, x1)


def _ple_kernel(x_ref, p_ref, g3_ref, wg_ref, wp_ref, gf_ref, y_ref, *, final_norm):
    x = x_ref[...]
    h3 = _rms(x, g3_ref[...]).astype(BF16)
    gate = jax.nn.sigmoid(_mm(h3, wg_ref[...]))
    x3 = x + _mm(p_ref[...].astype(BF16), wp_ref[...]) * gate
    y_ref[...] = _rms(x3, gf_ref[...]) if final_norm else x3


def _ple(x, p, g3, wg, wp, gf, final_norm, tm=512):
    t = x.shape[0]
    row = lambda c: pl.BlockSpec((tm, c), lambda i: (i, 0))
    return pl.pallas_call(
        functools.partial(_ple_kernel, final_norm=final_norm),
        grid=(t // tm,),
        in_specs=[row(D_MODEL), row(PLE_DIM)] + [_const_spec(a.shape) for a in (g3, wg, wp, gf)],
        out_specs=row(D_MODEL),
        out_shape=jax.ShapeDtypeStruct((t, D_MODEL), F32),
        compiler_params=_params("parallel"),
        name="ple",
    )(x, p, g3, wg, wp, gf)


def _t5_bucket(rel):
    nb = REL_BUCKETS // 2
    max_exact = nb // 2
    ret = (rel > 0).astype(jnp.int32) * nb
    n = jnp.abs(rel)
    nf = jnp.maximum(n, 1).astype(jnp.float32)
    large = max_exact + (jnp.log(nf / max_exact) / math.log(REL_MAX_DIST / max_exact)
                         * (nb - max_exact)).astype(jnp.int32)
    large = jnp.minimum(large, nb - 1)
    return ret + jnp.where(n < max_exact, n, large)


def _bias_table(rel_bias, tq):
    e = jnp.arange(5, dtype=jnp.int32)[:, None, None]
    r = jnp.arange(tq, dtype=jnp.int32)[None, :, None]
    c = jnp.arange(tq, dtype=jnp.int32)[None, None, :]
    bucket = _t5_bucket(c - r + (e - 2) * tq)[None]
    rb = rel_bias.astype(F32)
    tab = jnp.zeros((rb.shape[1], 5, tq, tq), F32)
    for k in range(REL_BUCKETS):
        tab = jnp.where(bucket == k, rb[k][:, None, None, None], tab)
    return tab


def _pack_in_proj(w_in, alpha_w, alpha_b):
    qa, ka, va, qg, kg, vg, rg, lr, gl = jnp.split(
        w_in, [512, 1024, 1536, 1792, 2048, 2560, 3072, 3104], axis=1)
    dup = lambda w: jnp.concatenate([w.reshape(-1, GLA_HEADS, 1, GLA_DK)] * 2, axis=2).reshape(-1, 512)
    lr_pad = jnp.pad(lr, ((0, 0), (0, LANES - lr.shape[1])))
    w = jnp.concatenate([qa * DA_HEAD_DIM ** -0.5, ka, va, dup(qg) * GLA_DK ** -0.5, dup(kg),
                         vg, rg, gl, lr_pad], axis=1).astype(BF16)
    aw = jnp.zeros((LANES, GLA_HEADS, 2, GLA_DK), F32)
    for d in range(2):
        aw = aw.at[d * GLA_GATE_RANK:(d + 1) * GLA_GATE_RANK, :, d, :].set(
            alpha_w[d].reshape(GLA_GATE_RANK, GLA_HEADS, GLA_DK))
    ab = alpha_b.reshape(2, GLA_HEADS, GLA_DK).transpose(1, 0, 2).reshape(1, 512)
    return w, aw.reshape(LANES, 512).astype(BF16), ab.astype(F32)


def _trunk(x, p, rel_bias, norm1_g, w_in, lambda_qk, da_norm_g, gla_alpha_w, gla_alpha_b, gla_norm_g,
           w_up_a, w_up_b, w_out, norm2_g, peer_w_q, peer_sub_keys, peer_u, peer_v, norm3_g,
           ple_w, ple_gate_w, final_norm_g, prepared):
    batch, seq, _ = x.shape
    depth = w_in.shape[0]
    xf = x.reshape(batch * seq, D_MODEL)
    row = lambda v: v.reshape(1, -1).astype(F32)
    for i in range(depth):
        w1, aw, ab, bias_tab, u_bf, vt_bf, sk = prepared[i]
        lam_init = 0.8 - 0.6 * math.exp(-0.3 * i)
        qa, ka, va, qg, kg, vg, rgs, gates, la = _in_proj(xf, row(norm1_g[i]), w1, aw, ab)
        oa = _diff_attn(qa, ka, va, bias_tab, lambda_qk[i].astype(F32), row(da_norm_g[i]),
                        batch, seq, lam_init)
        og = _gla(qg, kg, vg, la, rgs, row(gla_norm_g[i]), batch, seq)
        x1, h2, st = _mix(oa, og, gates, xf, w_up_a[i].astype(BF16), w_up_b[i].astype(BF16),
                          w_out[i].astype(BF16), row(norm2_g[i]), peer_w_q[i].astype(BF16), sk)
        r2, e2, n1, f1 = _peer_topk(st)
        x2 = _peer_dense(h2, u_bf, vt_bf, r2, e2, n1, f1, x1)
        xf = _ple(x2, p[i].reshape(batch * seq, PLE_DIM), row(norm3_g[i]), ple_gate_w[i].astype(BF16),
                  ple_w[i].astype(BF16), row(final_norm_g), final_norm=(i == depth - 1))
    return xf.reshape(batch, seq, D_MODEL)


def kernel(x_prompt, x_sample, p_prompt, p_sample, rel_bias, norm1_g, w_in, lambda_qk, da_norm_g,
           gla_alpha_w, gla_alpha_b, gla_norm_g, w_up_a, w_up_b, w_out, norm2_g, peer_w_q,
           peer_sub_keys, peer_u, peer_v, norm3_g, ple_w, ple_gate_w, final_norm_g):
    prepared = []
    for i in range(w_in.shape[0]):
        w1, aw, ab = _pack_in_proj(w_in[i], gla_alpha_w[i], gla_alpha_b[i])
        prepared.append((w1, aw, ab, _bias_table(rel_bias, ATT_TQ), peer_u[i].astype(BF16),
                         peer_v[i].T.astype(BF16),
                         peer_sub_keys[i].reshape(2 * PEER_HEADS, PEER_N_KEYS, -1).astype(BF16)))
    shared = (rel_bias, norm1_g, w_in, lambda_qk, da_norm_g, gla_alpha_w, gla_alpha_b, gla_norm_g,
              w_up_a, w_up_b, w_out, norm2_g, peer_w_q, peer_sub_keys, peer_u, peer_v, norm3_g,
              ple_w, ple_gate_w, final_norm_g)
    y_prompt = _trunk(x_prompt, p_prompt, *shared, prepared)
    y_sample = _trunk(x_sample, p_sample, *shared, prepared)
    return (y_prompt, y_sample)
```

```python
import functools
import math

import jax
import jax.numpy as jnp
from jax import lax
from jax.experimental import pallas as pl
from jax.experimental.pallas import tpu as pltpu

F32 = jnp.float32
BF16 = jnp.bfloat16

D_MODEL = 1024
DA_HEADS = 4
DA_HEAD_DIM = 64
DA_V_DIM = 128
GLA_HEADS = 4
GLA_DK = 64
GLA_DV = 128
GLA_GATE_RANK = 16
GLA_GATE_TAU = 16.0
GLA_CHUNK = 64
REL_BUCKETS = 32
REL_MAX_DIST = 128
PEER_HEADS = 8
PEER_N_KEYS = 128
PEER_TOPK = 16
PLE_DIM = 256
EPS = 1e-6

LANES = 128
V7X_VMEM_BYTES = 64 * 1024 * 1024
VMEM_LIMIT = V7X_VMEM_BYTES - 8 * 1024 * 1024

_C_QA, _C_KA, _C_VA, _C_QG, _C_KG, _C_VG, _C_RG, _C_GL, _C_LR, _C_END = (
    0, 512, 1024, 1536, 2048, 2560, 3072, 3584, 5632, 5760)

ATT_TQ = 256
ATT_RB = 128
LOG2E = math.log2(math.e)
PEER_TT = 512
PEER_EC = 2048
PEER_LW = 128


def _nt(a, b):
    return lax.dot_general(a, b, (((1,), (1,)), ((), ())), preferred_element_type=F32)


def _tn(a, b):
    return lax.dot_general(a, b, (((0,), (0,)), ((), ())), preferred_element_type=F32)


def _mm(a, b):
    return jnp.dot(a, b, preferred_element_type=F32)


def _rms(x, g):
    return x * lax.rsqrt(jnp.mean(x * x, axis=-1, keepdims=True) + EPS) * g


def _params(*sem):
    return pltpu.CompilerParams(dimension_semantics=sem, vmem_limit_bytes=VMEM_LIMIT)


def _const_spec(shape):
    nd = len(shape)
    return pl.BlockSpec(shape, lambda *_: (0,) * nd)


def _in_proj_kernel(x_ref, g_ref, w_ref, aw_ref, ab_ref,
                    qa_ref, ka_ref, va_ref, qg_ref, kg_ref, vg_ref, rg_ref, gate_ref, la_ref):
    h = _rms(x_ref[...], g_ref[...]).astype(BF16)

    def proj(lo, hi):
        return _mm(h, w_ref[:, lo:hi])

    qa_ref[...] = proj(_C_QA, _C_KA).astype(BF16)
    ka_ref[...] = proj(_C_KA, _C_VA).astype(BF16)
    va_ref[...] = proj(_C_VA, _C_QG).astype(BF16)
    qg_ref[...] = proj(_C_QG, _C_KG)
    kg_ref[...] = proj(_C_KG, _C_VG)
    vg_ref[...] = proj(_C_VG, _C_RG)
    rg = proj(_C_RG, _C_GL)
    rg_ref[...] = rg * jax.nn.sigmoid(rg)
    gate_ref[...] = jax.nn.sigmoid(proj(_C_GL, _C_LR))
    lr = proj(_C_LR, _C_END).astype(BF16)
    z = _mm(lr, aw_ref[...]) + ab_ref[...]
    la_ref[...] = (jnp.minimum(z, 0.0) - jnp.log1p(jnp.exp(-jnp.abs(z)))) * (1.0 / GLA_GATE_TAU)


def _in_proj(x, g, w, aw, ab, tm=256):
    t = x.shape[0]
    row = lambda c: pl.BlockSpec((tm, c), lambda i: (i, 0))
    outs = [(512, BF16)] * 3 + [(512, F32)] * 4 + [(2048, F32), (512, F32)]
    return pl.pallas_call(
        _in_proj_kernel,
        grid=(t // tm,),
        in_specs=[row(D_MODEL), _const_spec(g.shape), _const_spec(w.shape),
                  _const_spec(aw.shape), _const_spec(ab.shape)],
        out_specs=[row(c) for c, _ in outs],
        out_shape=[jax.ShapeDtypeStruct((t, c), dt) for c, dt in outs],
        compiler_params=_params("parallel"),
        name="in_proj",
    )(x, g, w, aw, ab)


def _attn_kernel(q_ref, k_ref, v_ref, bias_ref, lq_ref, g_ref, o_ref,
                 qb_s, s0_s, s1_s, p0_s, p1_s, a0_s, a1_s, m_s, l_s, acc_s, *, tq, nk, rb, lam_init):
    i = pl.program_id(2)
    q = q_ref[...]
    lane = lax.broadcasted_iota(jnp.int32, q.shape, 1)
    zero = jnp.zeros_like(q)
    qb_s[0:tq, :] = jnp.where(lane < DA_HEAD_DIM, q, zero)
    qb_s[tq:2 * tq, :] = jnp.where(lane >= DA_HEAD_DIM, q, zero)
    m_s[...] = jnp.full(m_s.shape, -jnp.inf, F32)
    l_s[...] = jnp.zeros(l_s.shape, F32)
    acc_s[...] = jnp.zeros(acc_s.shape, F32)
    s_bufs, p_bufs, a_bufs = (s0_s, s1_s), (p0_s, p1_s), (a0_s, a1_s)
    blocks = [slice(r0, r0 + rb) for r0 in range(0, 2 * tq, rb)]

    def scores(t, buf):
        kc = k_ref[pl.ds(pl.multiple_of(t * tq, tq), tq), :]
        bias = bias_ref[jnp.clip(t - i, -2, 2) + 2]
        s = _nt(qb_s[...], kc)
        s_bufs[buf][0:tq, :] = s[0:tq] + bias
        s_bufs[buf][tq:2 * tq, :] = s[tq:2 * tq] + bias

    def softmax(buf):
        for rows in blocks:
            s = s_bufs[buf][rows, :]
            m_old = m_s[rows, :]
            m_new = jnp.maximum(m_old, jnp.max(s, axis=-1, keepdims=True))
            alpha = jnp.exp2(m_old - m_new)
            ps = [jnp.exp2(s[:, c:c + LANES] - m_new) for c in range(0, tq, LANES)]
            psum = ps[0]
            for pp in ps[1:]:
                psum = psum + pp
            l_s[rows, :] = alpha * l_s[rows, :] + jnp.sum(psum, axis=-1, keepdims=True)
            m_s[rows, :] = m_new
            a_bufs[buf][rows, :] = alpha
            p_bufs[buf][rows, :] = jnp.concatenate([pp.astype(BF16) for pp in ps], axis=1)

    def values(t, buf):
        vc = v_ref[pl.ds(pl.multiple_of(t * tq, tq), tq), :]
        acc_s[...] = a_bufs[buf][...] * acc_s[...] + _mm(p_bufs[buf][...], vc)

    scores(0, 0)
    scores(1, 1)
    softmax(0)

    def steady(u, carry):
        t = 2 + 2 * u
        scores(t, 0)
        softmax(1)
        values(t - 2, 0)
        scores(t + 1, 1)
        softmax(0)
        values(t - 1, 1)
        return carry

    lax.fori_loop(0, (nk - 2) // 2, steady, 0)
    softmax(1)
    values(nk - 2, 0)
    values(nk - 1, 1)

    lq = lq_ref[...]
    lam = (jnp.exp(jnp.sum(lq[0:1] * lq[1:2], axis=-1, keepdims=True))
           - jnp.exp(jnp.sum(lq[2:3] * lq[3:4], axis=-1, keepdims=True)) + lam_init)
    o = acc_s[0:tq, :] / l_s[0:tq, :] - lam * (acc_s[tq:2 * tq, :] / l_s[tq:2 * tq, :])
    o_ref[...] = (_rms(o, g_ref[...]) * (1.0 - lam_init)).astype(BF16)


def _diff_attn(qa, ka, va, bias_tab, lq, g, batch, seq, lam_init):
    tq = ATT_TQ
    nq = seq // tq
    assert nq % 2 == 0 and nq >= 2
    stat = pltpu.VMEM((2 * tq, LANES), F32)
    return pl.pallas_call(
        functools.partial(_attn_kernel, tq=tq, nk=nq, rb=ATT_RB, lam_init=lam_init),
        grid=(batch, DA_HEADS, nq),
        in_specs=[
            pl.BlockSpec((tq, LANES), lambda b, h, i: (b * nq + i, h)),
            pl.BlockSpec((seq, LANES), lambda b, h, i: (b, h)),
            pl.BlockSpec((seq, LANES), lambda b, h, i: (b, h)),
            pl.BlockSpec((None, 5, tq, tq), lambda b, h, i: (h, 0, 0, 0)),
            _const_spec(lq.shape),
            _const_spec(g.shape),
        ],
        out_specs=pl.BlockSpec((tq, LANES), lambda b, h, i: (b * nq + i, h)),
        out_shape=jax.ShapeDtypeStruct(qa.shape, BF16),
        scratch_shapes=[pltpu.VMEM((2 * tq, LANES), BF16),
                        pltpu.VMEM((2 * tq, tq), F32), pltpu.VMEM((2 * tq, tq), F32),
                        pltpu.VMEM((2 * tq, tq), BF16), pltpu.VMEM((2 * tq, tq), BF16),
                        stat, stat, stat, stat,
                        pltpu.VMEM((2 * tq, DA_V_DIM), F32)],
        compiler_params=_params("parallel", "parallel", "arbitrary"),
        name="diff_attn",
    )(qa, ka, va, bias_tab, lq, g)


def _split3(x):
    hi = x.astype(BF16)
    r = x - hi.astype(F32)
    mid = r.astype(BF16)
    lo = (r - mid.astype(F32)).astype(BF16)
    return hi, mid, lo


def _gla_kernel(q_ref, k_ref, v_ref, la_ref, rg_ref, g_ref, o_ref,
                qe_s, kv_s, dec_s, oacc_s, *, nc, rows):
    c = GLA_CHUNK
    r_i = lax.broadcasted_iota(jnp.int32, (c, c), 0)
    c_i = lax.broadcasted_iota(jnp.int32, (c, c), 1)
    causal = c_i <= r_i
    pre = causal.astype(F32).astype(BF16)
    suf = (c_i >= r_i).astype(F32).astype(BF16)
    fwd_lane = lax.broadcasted_iota(jnp.int32, (c, LANES), 1) < GLA_DK

    def chunk(n, _):
        off = pl.multiple_of(n * c, c)
        la = la_ref[pl.ds(off, c), :]
        b = jnp.zeros((c, LANES), F32)
        for part in _split3(jnp.where(fwd_lane, la, 0.0)):
            b = b + _mm(pre, part)
        for part in _split3(jnp.where(fwd_lane, 0.0, la)):
            b = b + _mm(suf, part)
        b_last = jnp.where(fwd_lane[0:1], b[c - 1:c], b[0:1])
        q = q_ref[pl.ds(off, c), :]
        k = k_ref[pl.ds(off, c), :]
        vb = v_ref[pl.ds(off, c), :].astype(BF16)
        qe = q * jnp.exp(b)
        qe_s[pl.ds(off, c), :] = qe
        qeb = qe.astype(BF16)
        keb = (k * jnp.exp(-b)).astype(BF16)
        kdb = (k * jnp.exp(b_last - b)).astype(BF16)
        zero = jnp.zeros_like(qeb)
        att_f = _nt(jnp.where(fwd_lane, qeb, zero), keb)
        att_b = _nt(jnp.where(fwd_lane, zero, qeb), keb)
        att = jnp.where(causal, att_f, att_b)
        oacc_s[pl.ds(off, c), :] = _mm(att.astype(BF16), vb)
        kv_s[n] = _tn(vb, kdb)
        dec_s[n] = jnp.broadcast_to(jnp.exp(b_last), (8, LANES))
        return 0

    lax.fori_loop(0, nc, chunk, 0)

    def sweep(forward):
        def step(t, state):
            n = t if forward else nc - 1 - t
            off = pl.multiple_of(n * c, c)
            qe = qe_s[pl.ds(off, c), :]
            qm = jnp.where(fwd_lane, qe, 0.0) if forward else jnp.where(fwd_lane, 0.0, qe)
            oacc_s[pl.ds(off, c), :] += _nt(qm.astype(BF16), state.astype(BF16))
            return state * dec_s[n][0:1] + kv_s[n]
        lax.fori_loop(0, nc, step, jnp.zeros((GLA_DV, LANES), F32))

    sweep(True)
    sweep(False)

    def fin(r, _):
        off = pl.multiple_of(r * rows, rows)
        o = oacc_s[pl.ds(off, rows), :]
        o_ref[pl.ds(off, rows), :] = (_rms(o, g_ref[...]) * rg_ref[pl.ds(off, rows), :]).astype(BF16)
        return 0

    lax.fori_loop(0, (nc * c) // rows, fin, 0)


def _gla(qg, kg, vg, la, rgs, g, batch, seq):
    nc = seq // GLA_CHUNK
    blk = pl.BlockSpec((seq, LANES), lambda b, h: (b, h))
    return pl.pallas_call(
        functools.partial(_gla_kernel, nc=nc, rows=256),
        grid=(batch, GLA_HEADS),
        in_specs=[blk, blk, blk, blk, blk, _const_spec(g.shape)],
        out_specs=blk,
        out_shape=jax.ShapeDtypeStruct(vg.shape, BF16),
        scratch_shapes=[
            pltpu.VMEM((seq, LANES), F32),
            pltpu.VMEM((nc, GLA_DV, LANES), F32),
            pltpu.VMEM((nc, 8, LANES), F32),
            pltpu.VMEM((seq, LANES), F32),
        ],
        compiler_params=_params("parallel", "parallel"),
        name="gla",
    )(qg, kg, vg, la, rgs, g)


def _mix_kernel(oa_ref, og_ref, gate_ref, x_ref, wa_ref, wb_ref, wo_ref, g2_ref, wq_ref, sk_ref,
                x1_ref, h2_ref, st_ref):
    ya = _mm(oa_ref[...], wa_ref[...])
    yb = _mm(og_ref[...], wb_ref[...])
    merged = gate_ref[:, :D_MODEL] * ya + gate_ref[:, D_MODEL:] * yb
    x1 = x_ref[...] + _mm(merged.astype(BF16), wo_ref[...])
    x1_ref[...] = x1
    h2 = _rms(x1, g2_ref[...]).astype(BF16)
    h2_ref[...] = h2
    q = _mm(h2, wq_ref[...]).astype(BF16)
    for hc in range(2 * PEER_HEADS):
        st_ref[hc] = _nt(sk_ref[hc], q[:, hc * LANES:(hc + 1) * LANES])


def _mix(oa, og, gates, x, wa, wb, wo, g2, wq, sk, tm=256):
    t = x.shape[0]
    row = lambda c: pl.BlockSpec((tm, c), lambda i: (i, 0))
    return pl.pallas_call(
        _mix_kernel,
        grid=(t // tm,),
        in_specs=[row(512), row(512), row(2048), row(D_MODEL)]
        + [_const_spec(a.shape) for a in (wa, wb, wo, g2, wq, sk)],
        out_specs=[row(D_MODEL), row(D_MODEL),
                   pl.BlockSpec((2 * PEER_HEADS, PEER_N_KEYS, tm), lambda i: (0, 0, i))],
        out_shape=[jax.ShapeDtypeStruct((t, D_MODEL), F32),
                   jax.ShapeDtypeStruct((t, D_MODEL), BF16),
                   jax.ShapeDtypeStruct((2 * PEER_HEADS, PEER_N_KEYS, t), F32)],
        compiler_params=_params("parallel"),
        name="mix",
    )(oa, og, gates, x, wa, wb, wo, g2, wq, sk)


def _extract_top(cur, count, with_rank=False):
    rows = lax.broadcasted_iota(jnp.int32, cur.shape, 0)
    rank = jnp.full(cur.shape, float(count), F32)
    out = []
    for r in range(count):
        m = jnp.max(cur, axis=0, keepdims=True)
        out.append(m)
        first = jnp.min(jnp.where(cur == m, rows, cur.shape[0]), axis=0, keepdims=True)
        hit = rows == first
        cur = jnp.where(hit, -jnp.inf, cur)
        if with_rank:
            rank = jnp.where(hit, float(r), rank)
    return (out, rank) if with_rank else out


_PAIRS = [(a, b) for a in range(PEER_TOPK) for b in range(PEER_TOPK) if (a + 1) * (b + 1) <= PEER_TOPK]
_CAND_ROWS = -(-len(_PAIRS) // 8) * 8


_HI16 = 0xFFFF0000


def _bf16_hi_bits(x):
    u = lax.bitcast_convert_type(x, jnp.uint32)
    u = u + jnp.uint32(0x7FFF) + ((u >> 16) & jnp.uint32(1))
    return u & jnp.uint32(_HI16)


def _pack_pair(a, b):
    return _bf16_hi_bits(a) | (_bf16_hi_bits(b) >> 16)


def _sum_pair(w):
    return (lax.bitcast_convert_type(w & jnp.uint32(_HI16), F32)
            + lax.bitcast_convert_type(w << 16, F32))


def _topk_kernel(st_ref, r2_ref, e2_ref, n1_ref, f1_ref, cand_s):
    lanes = st_ref.shape[-1]
    pending = None
    for h in range(PEER_HEADS):
        s1 = st_ref[2 * h]
        s2 = st_ref[2 * h + 1]
        v1 = _extract_top(s1, PEER_TOPK)
        v2, rank2 = _extract_top(s2, PEER_TOPK, with_rank=True)
        cand_s[...] = jnp.full((_CAND_ROWS, lanes), -jnp.inf, F32)
        for r, (a, b) in enumerate(_PAIRS):
            cand_s[r:r + 1, :] = v1[a] + v2[b]
        top = _extract_top(cand_s[...], PEER_TOPK)
        thr = top[PEER_TOPK - 1]
        z = jnp.zeros_like(thr)
        for tk in top:
            z = z + jnp.exp(tk - top[0])
        n1 = jnp.zeros_like(s1)
        for b in range(PEER_TOPK):
            n1 = n1 + jnp.where(s1 + v2[b] >= thr, 1.0, 0.0)
        cur = (rank2, jnp.exp(s2 - v2[0]), n1, jnp.exp(s1 - v1[0]) / z)
        if h % 2 == 0:
            pending = cur
        else:
            for ref, a, b in zip((r2_ref, e2_ref, n1_ref, f1_ref), pending, cur):
                ref[h // 2] = _pack_pair(a, b)


def _peer_topk(st, tt=256):
    t = st.shape[-1]
    tab = pl.BlockSpec((PEER_HEADS // 2, PEER_N_KEYS, tt), lambda i: (0, 0, i))
    out = jax.ShapeDtypeStruct((PEER_HEADS // 2, PEER_N_KEYS, t), jnp.uint32)
    return pl.pallas_call(
        _topk_kernel,
        grid=(t // tt,),
        in_specs=[pl.BlockSpec((2 * PEER_HEADS, PEER_N_KEYS, tt), lambda i: (0, 0, i))],
        out_specs=[tab] * 4,
        out_shape=[out] * 4,
        scratch_shapes=[pltpu.VMEM((_CAND_ROWS, tt), F32)],
        compiler_params=_params("parallel"),
        name="peer_topk",
    )(st)


_GELU_C0 = math.sqrt(2.0 / math.pi)
_GELU_C1 = 0.044715 * _GELU_C0


def _gelu_tanh(x):
    hx = 0.5 * x
    return hx + hx * jnp.tanh(x * (_GELU_C0 + _GELU_C1 * (x * x)))


def _peer_kernel(h2_ref, u_ref, vt_ref, r2_ref, e2_ref, n1_ref, f1_ref, x1_ref, x2_ref,
                 acc_s, ut_s, wt_s, *, ec, lw):
    j = pl.program_id(1)
    tt = h2_ref.shape[0]
    zero = jnp.zeros((), BF16)

    @pl.when(j == 0)
    def _():
        acc_s[...] = jnp.zeros_like(acc_s)

    ut_s[...] = _nt(u_ref[...], h2_ref[...])
    for r0 in range(0, ec, PEER_N_KEYS):
        q = r0 // PEER_N_KEYS
        grp = j * (ec // PEER_N_KEYS // 8) + q // 8
        sub = slice(q % 8, q % 8 + 1)
        for c0 in range(0, tt, lw):
            cs = slice(c0, c0 + lw)
            gate = jnp.zeros((2 * PEER_N_KEYS, lw), BF16)
            for hp in range(PEER_HEADS // 2):
                row = lambda ref: pltpu.bitcast(
                    jnp.broadcast_to(ref[hp, grp, sub, cs], (PEER_N_KEYS, lw)), BF16)
                r2 = pltpu.bitcast(r2_ref[hp, :, cs], BF16)
                e2 = pltpu.bitcast(e2_ref[hp, :, cs], BF16)
                gate = gate + jnp.where(r2 < row(n1_ref), e2 * row(f1_ref), zero)
            g = _sum_pair(pltpu.bitcast(gate, jnp.uint32))
            act = _gelu_tanh(ut_s[r0:r0 + PEER_N_KEYS, cs])
            wt_s[r0:r0 + PEER_N_KEYS, cs] = (g * act).astype(BF16)
    acc_s[...] += _mm(vt_ref[...], wt_s[...])

    @pl.when(j == pl.num_programs(1) - 1)
    def _():
        x2_ref[...] = x1_ref[...] + acc_s[...].T


def _peer_dense(h2, u, vt, r2, e2, n1, f1, x1):
    t = h2.shape[0]
    tt, ec = PEER_TT, PEER_EC
    n_exp = u.shape[0]
    tab = pl.BlockSpec((PEER_HEADS // 2, PEER_N_KEYS, tt), lambda i, j: (0, 0, i))
    tab1 = pl.BlockSpec((PEER_HEADS // 2, PEER_N_KEYS // 8, 8, tt), lambda i, j: (0, 0, 0, i))
    n1 = n1.reshape(PEER_HEADS // 2, PEER_N_KEYS // 8, 8, t)
    f1 = f1.reshape(PEER_HEADS // 2, PEER_N_KEYS // 8, 8, t)
    return pl.pallas_call(
        functools.partial(_peer_kernel, ec=ec, lw=PEER_LW),
        grid=(t // tt, n_exp // ec),
        in_specs=[
            pl.BlockSpec((tt, D_MODEL), lambda i, j: (i, 0)),
            pl.BlockSpec((ec, D_MODEL), lambda i, j: (j, 0)),
            pl.BlockSpec((D_MODEL, ec), lambda i, j: (0, j)),
            tab, tab, tab1, tab1,
            pl.BlockSpec((tt, D_MODEL), lambda i, j: (i, 0)),
        ],
        out_specs=pl.BlockSpec((tt, D_MODEL), lambda i, j: (i, 0)),
        out_shape=jax.ShapeDtypeStruct((t, D_MODEL), F32),
        scratch_shapes=[pltpu.VMEM((D_MODEL, tt), F32),
                        pltpu.VMEM((ec, tt), F32),
                        pltpu.VMEM((ec, tt), BF16)],
        compiler_params=_params("parallel", "arbitrary"),
        name="peer_dense",
    )(h2, u, vt, r2, e2, n1, f1, x1)


def _ple_kernel(x_ref, p_ref, g3_ref, wg_ref, wp_ref, gf_ref, y_ref, *, final_norm):
    x = x_ref[...]
    h3 = _rms(x, g3_ref[...]).astype(BF16)
    gate = jax.nn.sigmoid(_mm(h3, wg_ref[...]))
    x3 = x + _mm(p_ref[...].astype(BF16), wp_ref[...]) * gate
    y_ref[...] = _rms(x3, gf_ref[...]) if final_norm else x3


def _ple(x, p, g3, wg, wp, gf, final_norm, tm=512):
    t = x.shape[0]
    row = lambda c: pl.BlockSpec((tm, c), lambda i: (i, 0))
    return pl.pallas_call(
        functools.partial(_ple_kernel, final_norm=final_norm),
        grid=(t // tm,),
        in_specs=[row(D_MODEL), row(PLE_DIM)] + [_const_spec(a.shape) for a in (g3, wg, wp, gf)],
        out_specs=row(D_MODEL),
        out_shape=jax.ShapeDtypeStruct((t, D_MODEL), F32),
        compiler_params=_params("parallel"),
        name="ple",
    )(x, p, g3, wg, wp, gf)


def _t5_bucket(rel):
    nb = REL_BUCKETS // 2
    max_exact = nb // 2
    ret = (rel > 0).astype(jnp.int32) * nb
    n = jnp.abs(rel)
    nf = jnp.maximum(n, 1).astype(jnp.float32)
    large = max_exact + (jnp.log(nf / max_exact) / math.log(REL_MAX_DIST / max_exact)
                         * (nb - max_exact)).astype(jnp.int32)
    large = jnp.minimum(large, nb - 1)
    return ret + jnp.where(n < max_exact, n, large)


def _bias_table(rel_bias, tq):
    e = jnp.arange(5, dtype=jnp.int32)[:, None, None]
    r = jnp.arange(tq, dtype=jnp.int32)[None, :, None]
    c = jnp.arange(tq, dtype=jnp.int32)[None, None, :]
    bucket = _t5_bucket(c - r + (e - 2) * tq)[None]
    rb = rel_bias.astype(F32) * LOG2E
    tab = jnp.zeros((rb.shape[1], 5, tq, tq), F32)
    for k in range(REL_BUCKETS):
        tab = jnp.where(bucket == k, rb[k][:, None, None, None], tab)
    return tab


def _pack_in_proj(w_in, alpha_w, alpha_b):
    qa, ka, va, qg, kg, vg, rg, lr, gl = jnp.split(
        w_in, [512, 1024, 1536, 1792, 2048, 2560, 3072, 3104], axis=1)
    dup = lambda w: jnp.concatenate([w.reshape(-1, GLA_HEADS, 1, GLA_DK)] * 2, axis=2).reshape(-1, 512)
    lr_pad = jnp.pad(lr, ((0, 0), (0, LANES - lr.shape[1])))
    w = jnp.concatenate([qa * (DA_HEAD_DIM ** -0.5 * LOG2E), ka, va, dup(qg) * GLA_DK ** -0.5, dup(kg),
                         vg, rg, gl, lr_pad], axis=1).astype(BF16)
    aw = jnp.zeros((LANES, GLA_HEADS, 2, GLA_DK), F32)
    for d in range(2):
        aw = aw.at[d * GLA_GATE_RANK:(d + 1) * GLA_GATE_RANK, :, d, :].set(
            alpha_w[d].reshape(GLA_GATE_RANK, GLA_HEADS, GLA_DK))
    ab = alpha_b.reshape(2, GLA_HEADS, GLA_DK).transpose(1, 0, 2).reshape(1, 512)
    return w, aw.reshape(LANES, 512).astype(BF16), ab.astype(F32)


def _trunk(x, p, rel_bias, norm1_g, w_in, lambda_qk, da_norm_g, gla_alpha_w, gla_alpha_b, gla_norm_g,
           w_up_a, w_up_b, w_out, norm2_g, peer_w_q, peer_sub_keys, peer_u, peer_v, norm3_g,
           ple_w, ple_gate_w, final_norm_g, prepared):
    batch, seq, _ = x.shape
    depth = w_in.shape[0]
    xf = x.reshape(batch * seq, D_MODEL)
    row = lambda v: v.reshape(1, -1).astype(F32)
    for i in range(depth):
        w1, aw, ab, bias_tab, u_bf, vt_bf, sk = prepared[i]
        lam_init = 0.8 - 0.6 * math.exp(-0.3 * i)
        qa, ka, va, qg, kg, vg, rgs, gates, la = _in_proj(xf, row(norm1_g[i]), w1, aw, ab)
        oa = _diff_attn(qa, ka, va, bias_tab, lambda_qk[i].astype(F32), row(da_norm_g[i]),
                        batch, seq, lam_init)
        og = _gla(qg, kg, vg, la, rgs, row(gla_norm_g[i]), batch, seq)
        x1, h2, st = _mix(oa, og, gates, xf, w_up_a[i].astype(BF16), w_up_b[i].astype(BF16),
                          w_out[i].astype(BF16), row(norm2_g[i]), peer_w_q[i].astype(BF16), sk)
        r2, e2, n1, f1 = _peer_topk(st)
        x2 = _peer_dense(h2, u_bf, vt_bf, r2, e2, n1, f1, x1)
        xf = _ple(x2, p[i].reshape(batch * seq, PLE_DIM), row(norm3_g[i]), ple_gate_w[i].astype(BF16),
                  ple_w[i].astype(BF16), row(final_norm_g), final_norm=(i == depth - 1))
    return xf.reshape(batch, seq, D_MODEL)


def kernel(x_prompt, x_sample, p_prompt, p_sample, rel_bias, norm1_g, w_in, lambda_qk, da_norm_g,
           gla_alpha_w, gla_alpha_b, gla_norm_g, w_up_a, w_up_b, w_out, norm2_g, peer_w_q,
           peer_sub_keys, peer_u, peer_v, norm3_g, ple_w, ple_gate_w, final_norm_g):
    prepared = []
    for i in range(w_in.shape[0]):
        w1, aw, ab = _pack_in_proj(w_in[i], gla_alpha_w[i], gla_alpha_b[i])
        prepared.append((w1, aw, ab, _bias_table(rel_bias, ATT_TQ), peer_u[i].astype(BF16),
                         peer_v[i].T.astype(BF16),
                         peer_sub_keys[i].reshape(2 * PEER_HEADS, PEER_N_KEYS, -1).astype(BF16)))
    shared = (rel_bias, norm1_g, w_in, lambda_qk, da_norm_g, gla_alpha_w, gla_alpha_b, gla_norm_g,
              w_up_a, w_up_b, w_out, norm2_g, peer_w_q, peer_sub_keys, peer_u, peer_v, norm3_g,
              ple_w, ple_gate_w, final_norm_g)
    y_prompt = _trunk(x_prompt, p_prompt, *shared, prepared)
    y_sample = _trunk(x_sample, p_sample, *shared, prepared)
    return (y_prompt, y_sample)
```

```python
import functools
import math

import jax
import jax.numpy as jnp
from jax import lax
from jax.experimental import pallas as pl
from jax.experimental.pallas import tpu as pltpu

F32 = jnp.float32
BF16 = jnp.bfloat16

D_MODEL = 1024
DA_HEADS = 4
DA_HEAD_DIM = 64
DA_V_DIM = 128
GLA_HEADS = 4
GLA_DK = 64
GLA_DV = 128
GLA_GATE_RANK = 16
GLA_GATE_TAU = 16.0
GLA_CHUNK = 64
REL_BUCKETS = 32
REL_MAX_DIST = 128
PEER_HEADS = 8
PEER_N_KEYS = 128
PEER_TOPK = 16
PLE_DIM = 256
EPS = 1e-6

LANES = 128
V7X_VMEM_BYTES = 64 * 1024 * 1024
VMEM_LIMIT = V7X_VMEM_BYTES - 8 * 1024 * 1024

_C_QA, _C_KA, _C_VA, _C_QG, _C_KG, _C_VG, _C_RG, _C_GL, _C_LR, _C_END = (
    0, 512, 1024, 1536, 2048, 2560, 3072, 3584, 5632, 5760)

GLA_UNROLL = 8
ATT_TQ = 256
ATT_RB = 128
LOG2E = math.log2(math.e)
PEER_TT = 512
PEER_EC = 2048
PEER_LW = 128


def _nt(a, b):
    return lax.dot_general(a, b, (((1,), (1,)), ((), ())), preferred_element_type=F32)


def _tn(a, b):
    return lax.dot_general(a, b, (((0,), (0,)), ((), ())), preferred_element_type=F32)


def _mm(a, b):
    return jnp.dot(a, b, preferred_element_type=F32)


def _rms(x, g):
    return x * lax.rsqrt(jnp.mean(x * x, axis=-1, keepdims=True) + EPS) * g


def _params(*sem):
    return pltpu.CompilerParams(dimension_semantics=sem, vmem_limit_bytes=VMEM_LIMIT)


def _const_spec(shape):
    nd = len(shape)
    return pl.BlockSpec(shape, lambda *_: (0,) * nd)


def _in_proj_kernel(x_ref, g_ref, w_ref, aw_ref, ab_ref,
                    qa_ref, ka_ref, va_ref, qg_ref, kg_ref, vg_ref, rg_ref, gate_ref, la_ref):
    h = _rms(x_ref[...], g_ref[...]).astype(BF16)

    def proj(lo, hi):
        return _mm(h, w_ref[:, lo:hi])

    qa_ref[...] = proj(_C_QA, _C_KA).astype(BF16)
    ka_ref[...] = proj(_C_KA, _C_VA).astype(BF16)
    va_ref[...] = proj(_C_VA, _C_QG).astype(BF16)
    qg_ref[...] = proj(_C_QG, _C_KG)
    kg_ref[...] = proj(_C_KG, _C_VG)
    vg_ref[...] = proj(_C_VG, _C_RG)
    rg = proj(_C_RG, _C_GL)
    rg_ref[...] = rg * jax.nn.sigmoid(rg)
    gate_ref[...] = jax.nn.sigmoid(proj(_C_GL, _C_LR))
    lr = proj(_C_LR, _C_END).astype(BF16)
    z = _mm(lr, aw_ref[...]) + ab_ref[...]
    la_ref[...] = (jnp.minimum(z, 0.0) - jnp.log1p(jnp.exp(-jnp.abs(z)))) * (1.0 / GLA_GATE_TAU)


def _in_proj(x, g, w, aw, ab, tm=256):
    t = x.shape[0]
    row = lambda c: pl.BlockSpec((tm, c), lambda i: (i, 0))
    outs = [(512, BF16)] * 3 + [(512, F32)] * 4 + [(2048, F32), (512, F32)]
    return pl.pallas_call(
        _in_proj_kernel,
        grid=(t // tm,),
        in_specs=[row(D_MODEL), _const_spec(g.shape), _const_spec(w.shape),
                  _const_spec(aw.shape), _const_spec(ab.shape)],
        out_specs=[row(c) for c, _ in outs],
        out_shape=[jax.ShapeDtypeStruct((t, c), dt) for c, dt in outs],
        compiler_params=_params("parallel"),
        name="in_proj",
    )(x, g, w, aw, ab)


def _attn_kernel(q_ref, k_ref, v_ref, bias_ref, lq_ref, g_ref, o_ref,
                 qb_s, s0_s, s1_s, p0_s, p1_s, a0_s, a1_s, m_s, l_s, acc_s, *, tq, nk, rb, lam_init):
    i = pl.program_id(2)
    q = q_ref[...]
    lane = lax.broadcasted_iota(jnp.int32, q.shape, 1)
    zero = jnp.zeros_like(q)
    qb_s[0:tq, :] = jnp.where(lane < DA_HEAD_DIM, q, zero)
    qb_s[tq:2 * tq, :] = jnp.where(lane >= DA_HEAD_DIM, q, zero)
    m_s[...] = jnp.full(m_s.shape, -jnp.inf, F32)
    l_s[...] = jnp.zeros(l_s.shape, F32)
    acc_s[...] = jnp.zeros(acc_s.shape, F32)
    s_bufs, p_bufs, a_bufs = (s0_s, s1_s), (p0_s, p1_s), (a0_s, a1_s)
    blocks = [slice(r0, r0 + rb) for r0 in range(0, 2 * tq, rb)]

    def scores(t, buf):
        kc = k_ref[pl.ds(pl.multiple_of(t * tq, tq), tq), :]
        bias = bias_ref[jnp.clip(t - i, -2, 2) + 2]
        s = _nt(qb_s[...], kc)
        s_bufs[buf][0:tq, :] = s[0:tq] + bias
        s_bufs[buf][tq:2 * tq, :] = s[tq:2 * tq] + bias

    def softmax(buf):
        for rows in blocks:
            s = s_bufs[buf][rows, :]
            m_old = m_s[rows, :]
            m_new = jnp.maximum(m_old, jnp.max(s, axis=-1, keepdims=True))
            alpha = jnp.exp2(m_old - m_new)
            ps = [jnp.exp2(s[:, c:c + LANES] - m_new) for c in range(0, tq, LANES)]
            psum = ps[0]
            for pp in ps[1:]:
                psum = psum + pp
            l_s[rows, :] = alpha * l_s[rows, :] + jnp.sum(psum, axis=-1, keepdims=True)
            m_s[rows, :] = m_new
            a_bufs[buf][rows, :] = alpha
            p_bufs[buf][rows, :] = jnp.concatenate([pp.astype(BF16) for pp in ps], axis=1)

    def values(t, buf):
        vc = v_ref[pl.ds(pl.multiple_of(t * tq, tq), tq), :]
        acc_s[...] = a_bufs[buf][...] * acc_s[...] + _mm(p_bufs[buf][...], vc)

    scores(0, 0)
    scores(1, 1)
    softmax(0)

    def steady(u, carry):
        t = 2 + 2 * u
        scores(t, 0)
        softmax(1)
        values(t - 2, 0)
        scores(t + 1, 1)
        softmax(0)
        values(t - 1, 1)
        return carry

    lax.fori_loop(0, (nk - 2) // 2, steady, 0)
    softmax(1)
    values(nk - 2, 0)
    values(nk - 1, 1)

    lq = lq_ref[...]
    lam = (jnp.exp(jnp.sum(lq[0:1] * lq[1:2], axis=-1, keepdims=True))
           - jnp.exp(jnp.sum(lq[2:3] * lq[3:4], axis=-1, keepdims=True)) + lam_init)
    o = acc_s[0:tq, :] / l_s[0:tq, :] - lam * (acc_s[tq:2 * tq, :] / l_s[tq:2 * tq, :])
    o_ref[...] = (_rms(o, g_ref[...]) * (1.0 - lam_init)).astype(BF16)


def _diff_attn(qa, ka, va, bias_tab, lq, g, batch, seq, lam_init):
    tq = ATT_TQ
    nq = seq // tq
    assert nq % 2 == 0 and nq >= 2
    stat = pltpu.VMEM((2 * tq, LANES), F32)
    return pl.pallas_call(
        functools.partial(_attn_kernel, tq=tq, nk=nq, rb=ATT_RB, lam_init=lam_init),
        grid=(batch, DA_HEADS, nq),
        in_specs=[
            pl.BlockSpec((tq, LANES), lambda b, h, i: (b * nq + i, h)),
            pl.BlockSpec((seq, LANES), lambda b, h, i: (b, h)),
            pl.BlockSpec((seq, LANES), lambda b, h, i: (b, h)),
            pl.BlockSpec((None, 5, tq, tq), lambda b, h, i: (h, 0, 0, 0)),
            _const_spec(lq.shape),
            _const_spec(g.shape),
        ],
        out_specs=pl.BlockSpec((tq, LANES), lambda b, h, i: (b * nq + i, h)),
        out_shape=jax.ShapeDtypeStruct(qa.shape, BF16),
        scratch_shapes=[pltpu.VMEM((2 * tq, LANES), BF16),
                        pltpu.VMEM((2 * tq, tq), F32), pltpu.VMEM((2 * tq, tq), F32),
                        pltpu.VMEM((2 * tq, tq), BF16), pltpu.VMEM((2 * tq, tq), BF16),
                        stat, stat, stat, stat,
                        pltpu.VMEM((2 * tq, DA_V_DIM), F32)],
        compiler_params=_params("parallel", "parallel", "arbitrary"),
        name="diff_attn",
    )(qa, ka, va, bias_tab, lq, g)


def _split3(x):
    hi = x.astype(BF16)
    r = x - hi.astype(F32)
    mid = r.astype(BF16)
    lo = (r - mid.astype(F32)).astype(BF16)
    return hi, mid, lo


def _gla_kernel(q_ref, k_ref, v_ref, la_ref, rg_ref, g_ref, o_ref,
                qe_s, kv_s, dec_s, oacc_s, *, nc, rows):
    c = GLA_CHUNK
    r_i = lax.broadcasted_iota(jnp.int32, (c, c), 0)
    c_i = lax.broadcasted_iota(jnp.int32, (c, c), 1)
    causal = c_i <= r_i
    pre = causal.astype(F32).astype(BF16)
    suf = (c_i >= r_i).astype(F32).astype(BF16)
    fwd_lane = lax.broadcasted_iota(jnp.int32, (c, LANES), 1) < GLA_DK

    def chunk(n, _):
        off = pl.multiple_of(n * c, c)
        la = la_ref[pl.ds(off, c), :]
        b = jnp.zeros((c, LANES), F32)
        for part in _split3(jnp.where(fwd_lane, la, 0.0)):
            b = b + _mm(pre, part)
        for part in _split3(jnp.where(fwd_lane, 0.0, la)):
            b = b + _mm(suf, part)
        b_last = jnp.where(fwd_lane[0:1], b[c - 1:c], b[0:1])
        q = q_ref[pl.ds(off, c), :]
        k = k_ref[pl.ds(off, c), :]
        vb = v_ref[pl.ds(off, c), :].astype(BF16)
        qe = q * jnp.exp(b)
        qe_s[pl.ds(off, c), :] = qe
        qeb = qe.astype(BF16)
        keb = (k * jnp.exp(-b)).astype(BF16)
        kdb = (k * jnp.exp(b_last - b)).astype(BF16)
        zero = jnp.zeros_like(qeb)
        att_f = _nt(jnp.where(fwd_lane, qeb, zero), keb)
        att_b = _nt(jnp.where(fwd_lane, zero, qeb), keb)
        att = jnp.where(causal, att_f, att_b)
        oacc_s[pl.ds(off, c), :] = _mm(att.astype(BF16), vb)
        kv_s[n] = _tn(vb, kdb)
        dec_s[n] = jnp.broadcast_to(jnp.exp(b_last), (8, LANES))
        return 0

    lax.fori_loop(0, nc, chunk, 0, unroll=GLA_UNROLL)

    def sweep(forward):
        def step(t, state):
            n = t if forward else nc - 1 - t
            off = pl.multiple_of(n * c, c)
            qe = qe_s[pl.ds(off, c), :]
            qm = jnp.where(fwd_lane, qe, 0.0) if forward else jnp.where(fwd_lane, 0.0, qe)
            oacc_s[pl.ds(off, c), :] += _nt(qm.astype(BF16), state.astype(BF16))
            return state * dec_s[n][0:1] + kv_s[n]
        lax.fori_loop(0, nc, step, jnp.zeros((GLA_DV, LANES), F32), unroll=GLA_UNROLL)

    sweep(True)
    sweep(False)

    def fin(r, _):
        off = pl.multiple_of(r * rows, rows)
        o = oacc_s[pl.ds(off, rows), :]
        o_ref[pl.ds(off, rows), :] = (_rms(o, g_ref[...]) * rg_ref[pl.ds(off, rows), :]).astype(BF16)
        return 0

    lax.fori_loop(0, (nc * c) // rows, fin, 0)


def _gla(qg, kg, vg, la, rgs, g, batch, seq):
    nc = seq // GLA_CHUNK
    blk = pl.BlockSpec((seq, LANES), lambda b, h: (b, h))
    return pl.pallas_call(
        functools.partial(_gla_kernel, nc=nc, rows=256),
        grid=(batch, GLA_HEADS),
        in_specs=[blk, blk, blk, blk, blk, _const_spec(g.shape)],
        out_specs=blk,
        out_shape=jax.ShapeDtypeStruct(vg.shape, BF16),
        scratch_shapes=[
            pltpu.VMEM((seq, LANES), F32),
            pltpu.VMEM((nc, GLA_DV, LANES), F32),
            pltpu.VMEM((nc, 8, LANES), F32),
            pltpu.VMEM((seq, LANES), F32),
        ],
        compiler_params=_params("parallel", "parallel"),
        name="gla",
    )(qg, kg, vg, la, rgs, g)


def _mix_kernel(oa_ref, og_ref, gate_ref, x_ref, wa_ref, wb_ref, wo_ref, g2_ref, wq_ref, sk_ref,
                x1_ref, h2_ref, st_ref):
    ya = _mm(oa_ref[...], wa_ref[...])
    yb = _mm(og_ref[...], wb_ref[...])
    merged = gate_ref[:, :D_MODEL] * ya + gate_ref[:, D_MODEL:] * yb
    x1 = x_ref[...] + _mm(merged.astype(BF16), wo_ref[...])
    x1_ref[...] = x1
    h2 = _rms(x1, g2_ref[...]).astype(BF16)
    h2_ref[...] = h2
    q = _mm(h2, wq_ref[...]).astype(BF16)
    for hc in range(2 * PEER_HEADS):
        st_ref[hc] = _nt(sk_ref[hc], q[:, hc * LANES:(hc + 1) * LANES])


def _mix(oa, og, gates, x, wa, wb, wo, g2, wq, sk, tm=256):
    t = x.shape[0]
    row = lambda c: pl.BlockSpec((tm, c), lambda i: (i, 0))
    return pl.pallas_call(
        _mix_kernel,
        grid=(t // tm,),
        in_specs=[row(512), row(512), row(2048), row(D_MODEL)]
        + [_const_spec(a.shape) for a in (wa, wb, wo, g2, wq, sk)],
        out_specs=[row(D_MODEL), row(D_MODEL),
                   pl.BlockSpec((2 * PEER_HEADS, PEER_N_KEYS, tm), lambda i: (0, 0, i))],
        out_shape=[jax.ShapeDtypeStruct((t, D_MODEL), F32),
                   jax.ShapeDtypeStruct((t, D_MODEL), BF16),
                   jax.ShapeDtypeStruct((2 * PEER_HEADS, PEER_N_KEYS, t), F32)],
        compiler_params=_params("parallel"),
        name="mix",
    )(oa, og, gates, x, wa, wb, wo, g2, wq, sk)


def _extract_top(cur, count, with_rank=False, tie_safe=True):
    rows = lax.broadcasted_iota(jnp.int32, cur.shape, 0)
    rank = jnp.full(cur.shape, float(count), F32)
    out = []
    for r in range(count):
        m = jnp.max(cur, axis=0, keepdims=True)
        out.append(m)
        hit = cur == m
        if tie_safe:
            hit = rows == jnp.min(jnp.where(hit, rows, cur.shape[0]), axis=0, keepdims=True)
        cur = jnp.where(hit, -jnp.inf, cur)
        if with_rank:
            rank = jnp.where(hit, float(r), rank)
    return (out, rank) if with_rank else out


_PAIRS = [(a, b) for a in range(PEER_TOPK) for b in range(PEER_TOPK) if (a + 1) * (b + 1) <= PEER_TOPK]
_CAND_ROWS = -(-len(_PAIRS) // 8) * 8


_HI16 = 0xFFFF0000


def _bf16_hi_bits(x):
    u = lax.bitcast_convert_type(x, jnp.uint32)
    u = u + jnp.uint32(0x7FFF) + ((u >> 16) & jnp.uint32(1))
    return u & jnp.uint32(_HI16)


def _pack_pair(a, b):
    return _bf16_hi_bits(a) | (_bf16_hi_bits(b) >> 16)


def _sum_pair(w):
    return (lax.bitcast_convert_type(w & jnp.uint32(_HI16), F32)
            + lax.bitcast_convert_type(w << 16, F32))


def _head_tables(s1, s2, cand_s, tie_safe):
    k = PEER_TOPK
    lanes = s1.shape[-1]
    v1 = _extract_top(s1, k, tie_safe=tie_safe)
    v2, rank2 = _extract_top(s2, k, with_rank=True, tie_safe=tie_safe)
    cand_s[...] = jnp.full((_CAND_ROWS, lanes), -jnp.inf, F32)
    for r, (a, b) in enumerate(_PAIRS):
        cand_s[r:r + 1, :] = v1[a] + v2[b]
    cand = cand_s[...]
    top = _extract_top(cand, k, tie_safe=tie_safe)
    thr = top[k - 1]
    z = jnp.zeros_like(thr)
    for tk in top:
        z = z + jnp.exp(tk - top[0])
    count = lambda cond: jnp.sum(jnp.where(cond, 1.0, 0.0), axis=0, keepdims=True)
    if tie_safe:
        n1 = jnp.zeros_like(s1)
        for b in range(k):
            n1 = n1 + jnp.where(s1 + v2[b] >= thr, 1.0, 0.0)
        bad = None
    else:
        n1 = jnp.zeros_like(s1)
        for b in range(4):
            n1 = n1 + jnp.where(s1 + v2[b] >= thr, 1.0, 0.0)
        for a in range(3):
            n_a = jnp.zeros_like(thr)
            for b in range(k // (a + 1)):
                n_a = n_a + jnp.where(v1[a] + v2[b] >= thr, 1.0, 0.0)
            n1 = jnp.where(s1 == v1[a], n_a, n1)
        bad = ((count(s1 >= v1[k - 1]) != k) | (count(s2 >= v2[k - 1]) != k)
               | (count(cand >= thr) != k))
    return (rank2, jnp.exp(s2 - v2[0]), n1, jnp.exp(s1 - v1[0]) / z), bad


def _topk_kernel(st_ref, r2_ref, e2_ref, n1_ref, f1_ref, cand_s):
    refs = (r2_ref, e2_ref, n1_ref, f1_ref)
    for hp in range(PEER_HEADS // 2):
        def tables(tie_safe):
            ta, bad_a = _head_tables(st_ref[4 * hp], st_ref[4 * hp + 1], cand_s, tie_safe)
            tb, bad_b = _head_tables(st_ref[4 * hp + 2], st_ref[4 * hp + 3], cand_s, tie_safe)
            for ref, a, b in zip(refs, ta, tb):
                ref[hp] = _pack_pair(a, b)
            return None if tie_safe else jnp.max(jnp.where(bad_a | bad_b, 1.0, 0.0))

        any_bad = tables(tie_safe=False)

        @pl.when(any_bad > 0.0)
        def _():
            tables(tie_safe=True)


def _peer_topk(st, tt=256):
    t = st.shape[-1]
    tab = pl.BlockSpec((PEER_HEADS // 2, PEER_N_KEYS, tt), lambda i: (0, 0, i))
    out = jax.ShapeDtypeStruct((PEER_HEADS // 2, PEER_N_KEYS, t), jnp.uint32)
    return pl.pallas_call(
        _topk_kernel,
        grid=(t // tt,),
        in_specs=[pl.BlockSpec((2 * PEER_HEADS, PEER_N_KEYS, tt), lambda i: (0, 0, i))],
        out_specs=[tab] * 4,
        out_shape=[out] * 4,
        scratch_shapes=[pltpu.VMEM((_CAND_ROWS, tt), F32)],
        compiler_params=_params("parallel"),
        name="peer_topk",
    )(st)


_GELU_C0 = math.sqrt(2.0 / math.pi)
_GELU_C1 = 0.044715 * _GELU_C0


def _gelu_tanh(x):
    hx = 0.5 * x
    return hx + hx * jnp.tanh(x * (_GELU_C0 + _GELU_C1 * (x * x)))


def _peer_kernel(h2_ref, u_ref, vt_ref, r2_ref, e2_ref, n1_ref, f1_ref, x1_ref, x2_ref,
                 acc_s, ut_s, wt_s, *, ec, lw):
    j = pl.program_id(1)
    tt = h2_ref.shape[0]
    zero = jnp.zeros((), BF16)

    @pl.when(j == 0)
    def _():
        acc_s[...] = jnp.zeros_like(acc_s)

    ut_s[...] = _nt(u_ref[...], h2_ref[...])
    for r0 in range(0, ec, PEER_N_KEYS):
        q = r0 // PEER_N_KEYS
        grp = j * (ec // PEER_N_KEYS // 8) + q // 8
        sub = slice(q % 8, q % 8 + 1)
        for c0 in range(0, tt, lw):
            cs = slice(c0, c0 + lw)
            gate = jnp.zeros((2 * PEER_N_KEYS, lw), BF16)
            for hp in range(PEER_HEADS // 2):
                row = lambda ref: pltpu.bitcast(
                    jnp.broadcast_to(ref[hp, grp, sub, cs], (PEER_N_KEYS, lw)), BF16)
                r2 = pltpu.bitcast(r2_ref[hp, :, cs], BF16)
                e2 = pltpu.bitcast(e2_ref[hp, :, cs], BF16)
                gate = gate + jnp.where(r2 < row(n1_ref), e2 * row(f1_ref), zero)
            g = _sum_pair(pltpu.bitcast(gate, jnp.uint32))
            act = _gelu_tanh(ut_s[r0:r0 + PEER_N_KEYS, cs].astype(BF16))
            wt_s[r0:r0 + PEER_N_KEYS, cs] = g.astype(BF16) * act
    acc_s[...] += _mm(vt_ref[...], wt_s[...])

    @pl.when(j == pl.num_programs(1) - 1)
    def _():
        x2_ref[...] = x1_ref[...] + acc_s[...].T


def _peer_dense(h2, u, vt, r2, e2, n1, f1, x1):
    t = h2.shape[0]
    tt, ec = PEER_TT, PEER_EC
    n_exp = u.shape[0]
    tab = pl.BlockSpec((PEER_HEADS // 2, PEER_N_KEYS, tt), lambda i, j: (0, 0, i))
    tab1 = pl.BlockSpec((PEER_HEADS // 2, PEER_N_KEYS // 8, 8, tt), lambda i, j: (0, 0, 0, i))
    n1 = n1.reshape(PEER_HEADS // 2, PEER_N_KEYS // 8, 8, t)
    f1 = f1.reshape(PEER_HEADS // 2, PEER_N_KEYS // 8, 8, t)
    return pl.pallas_call(
        functools.partial(_peer_kernel, ec=ec, lw=PEER_LW),
        grid=(t // tt, n_exp // ec),
        in_specs=[
            pl.BlockSpec((tt, D_MODEL), lambda i, j: (i, 0)),
            pl.BlockSpec((ec, D_MODEL), lambda i, j: (j, 0)),
            pl.BlockSpec((D_MODEL, ec), lambda i, j: (0, j)),
            tab, tab, tab1, tab1,
            pl.BlockSpec((tt, D_MODEL), lambda i, j: (i, 0)),
        ],
        out_specs=pl.BlockSpec((tt, D_MODEL), lambda i, j: (i, 0)),
        out_shape=jax.ShapeDtypeStruct((t, D_MODEL), F32),
        scratch_shapes=[pltpu.VMEM((D_MODEL, tt), F32),
                        pltpu.VMEM((ec, tt), F32),
                        pltpu.VMEM((ec, tt), BF16)],
        compiler_params=_params("parallel", "arbitrary"),
        name="peer_dense",
    )(h2, u, vt, r2, e2, n1, f1, x1)


def _ple_kernel(x_ref, p_ref, g3_ref, wg_ref, wp_ref, gf_ref, y_ref, *, final_norm):
    x = x_ref[...]
    h3 = _rms(x, g3_ref[...]).astype(BF16)
    gate = jax.nn.sigmoid(_mm(h3, wg_ref[...]))
    x3 = x + _mm(p_ref[...].astype(BF16), wp_ref[...]) * gate
    y_ref[...] = _rms(x3, gf_ref[...]) if final_norm else x3


def _ple(x, p, g3, wg, wp, gf, final_norm, tm=512):
    t = x.shape[0]
    row = lambda c: pl.BlockSpec((tm, c), lambda i: (i, 0))
    return pl.pallas_call(
        functools.partial(_ple_kernel, final_norm=final_norm),
        grid=(t // tm,),
        in_specs=[row(D_MODEL), row(PLE_DIM)] + [_const_spec(a.shape) for a in (g3, wg, wp, gf)],
        out_specs=row(D_MODEL),
        out_shape=jax.ShapeDtypeStruct((t, D_MODEL), F32),
        compiler_params=_params("parallel"),
        name="ple",
    )(x, p, g3, wg, wp, gf)


def _t5_bucket(rel):
    nb = REL_BUCKETS // 2
    max_exact = nb // 2
    ret = (rel > 0).astype(jnp.int32) * nb
    n = jnp.abs(rel)
    nf = jnp.maximum(n, 1).astype(jnp.float32)
    large = max_exact + (jnp.log(nf / max_exact) / math.log(REL_MAX_DIST / max_exact)
                         * (nb - max_exact)).astype(jnp.int32)
    large = jnp.minimum(large, nb - 1)
    return ret + jnp.where(n < max_exact, n, large)


def _bias_table(rel_bias, tq):
    e = jnp.arange(5, dtype=jnp.int32)[:, None, None]
    r = jnp.arange(tq, dtype=jnp.int32)[None, :, None]
    c = jnp.arange(tq, dtype=jnp.int32)[None, None, :]
    bucket = _t5_bucket(c - r + (e - 2) * tq)[None]
    rb = rel_bias.astype(F32) * LOG2E
    tab = jnp.zeros((rb.shape[1], 5, tq, tq), F32)
    for k in range(REL_BUCKETS):
        tab = jnp.where(bucket == k, rb[k][:, None, None, None], tab)
    return tab


def _pack_in_proj(w_in, alpha_w, alpha_b):
    qa, ka, va, qg, kg, vg, rg, lr, gl = jnp.split(
        w_in, [512, 1024, 1536, 1792, 2048, 2560, 3072, 3104], axis=1)
    dup = lambda w: jnp.concatenate([w.reshape(-1, GLA_HEADS, 1, GLA_DK)] * 2, axis=2).reshape(-1, 512)
    lr_pad = jnp.pad(lr, ((0, 0), (0, LANES - lr.shape[1])))
    w = jnp.concatenate([qa * (DA_HEAD_DIM ** -0.5 * LOG2E), ka, va, dup(qg) * GLA_DK ** -0.5, dup(kg),
                         vg, rg, gl, lr_pad], axis=1).astype(BF16)
    aw = jnp.zeros((LANES, GLA_HEADS, 2, GLA_DK), F32)
    for d in range(2):
        aw = aw.at[d * GLA_GATE_RANK:(d + 1) * GLA_GATE_RANK, :, d, :].set(
            alpha_w[d].reshape(GLA_GATE_RANK, GLA_HEADS, GLA_DK))
    ab = alpha_b.reshape(2, GLA_HEADS, GLA_DK).transpose(1, 0, 2).reshape(1, 512)
    return w, aw.reshape(LANES, 512).astype(BF16), ab.astype(F32)


def _trunk(x, p, rel_bias, norm1_g, w_in, lambda_qk, da_norm_g, gla_alpha_w, gla_alpha_b, gla_norm_g,
           w_up_a, w_up_b, w_out, norm2_g, peer_w_q, peer_sub_keys, peer_u, peer_v, norm3_g,
           ple_w, ple_gate_w, final_norm_g, prepared):
    batch, seq, _ = x.shape
    depth = w_in.shape[0]
    xf = x.reshape(batch * seq, D_MODEL)
    row = lambda v: v.reshape(1, -1).astype(F32)
    for i in range(depth):
        w1, aw, ab, bias_tab, u_bf, vt_bf, sk = prepared[i]
        lam_init = 0.8 - 0.6 * math.exp(-0.3 * i)
        qa, ka, va, qg, kg, vg, rgs, gates, la = _in_proj(xf, row(norm1_g[i]), w1, aw, ab)
        oa = _diff_attn(qa, ka, va, bias_tab, lambda_qk[i].astype(F32), row(da_norm_g[i]),
                        batch, seq, lam_init)
        og = _gla(qg, kg, vg, la, rgs, row(gla_norm_g[i]), batch, seq)
        x1, h2, st = _mix(oa, og, gates, xf, w_up_a[i].astype(BF16), w_up_b[i].astype(BF16),
                          w_out[i].astype(BF16), row(norm2_g[i]), peer_w_q[i].astype(BF16), sk)
        r2, e2, n1, f1 = _peer_topk(st)
        x2 = _peer_dense(h2, u_bf, vt_bf, r2, e2, n1, f1, x1)
        xf = _ple(x2, p[i].reshape(batch * seq, PLE_DIM), row(norm3_g[i]), ple_gate_w[i].astype(BF16),
                  ple_w[i].astype(BF16), row(final_norm_g), final_norm=(i == depth - 1))
    return xf.reshape(batch, seq, D_MODEL)


def kernel(x_prompt, x_sample, p_prompt, p_sample, rel_bias, norm1_g, w_in, lambda_qk, da_norm_g,
           gla_alpha_w, gla_alpha_b, gla_norm_g, w_up_a, w_up_b, w_out, norm2_g, peer_w_q,
           peer_sub_keys, peer_u, peer_v, norm3_g, ple_w, ple_gate_w, final_norm_g):
    prepared = []
    for i in range(w_in.shape[0]):
        w1, aw, ab = _pack_in_proj(w_in[i], gla_alpha_w[i], gla_alpha_b[i])
        prepared.append((w1, aw, ab, _bias_table(rel_bias, ATT_TQ), peer_u[i].astype(BF16),
                         peer_v[i].T.astype(BF16),
                         peer_sub_keys[i].reshape(2 * PEER_HEADS, PEER_N_KEYS, -1).astype(BF16)))
    shared = (rel_bias, norm1_g, w_in, lambda_qk, da_norm_g, gla_alpha_w, gla_alpha_b, gla_norm_g,
              w_up_a, w_up_b, w_out, norm2_g, peer_w_q, peer_sub_keys, peer_u, peer_v, norm3_g,
              ple_w, ple_gate_w, final_norm_g)
    y_prompt = _trunk(x_prompt, p_prompt, *shared, prepared)
    y_sample = _trunk(x_sample, p_sample, *shared, prepared)
    return (y_prompt, y_sample)
```

```python
import functools
import math

import jax
import jax.numpy as jnp
from jax import lax
from jax.experimental import pallas as pl
from jax.experimental.pallas import tpu as pltpu

F32 = jnp.float32
BF16 = jnp.bfloat16

D_MODEL = 1024
DA_HEADS = 4
DA_HEAD_DIM = 64
DA_V_DIM = 128
GLA_HEADS = 4
GLA_DK = 64
GLA_DV = 128
GLA_GATE_RANK = 16
GLA_GATE_TAU = 16.0
GLA_CHUNK = 64
REL_BUCKETS = 32
REL_MAX_DIST = 128
PEER_HEADS = 8
PEER_N_KEYS = 128
PEER_TOPK = 16
PLE_DIM = 256
EPS = 1e-6

LANES = 128
V7X_VMEM_BYTES = 64 * 1024 * 1024
VMEM_LIMIT = V7X_VMEM_BYTES - 8 * 1024 * 1024

_C_QA, _C_KA, _C_VA, _C_QG, _C_KG, _C_VG, _C_RG, _C_GL, _C_LR, _C_END = (
    0, 512, 1024, 1536, 2048, 2560, 3072, 3584, 5632, 5760)

GLA_UNROLL = 8
GLA_GROUP = 4
ATT_TQ = 256
ATT_RB = 128
LOG2E = math.log2(math.e)
PEER_TT = 512
PEER_EC = 2048
PEER_LW = 128


def _nt(a, b):
    return lax.dot_general(a, b, (((1,), (1,)), ((), ())), preferred_element_type=F32)


def _tn(a, b):
    return lax.dot_general(a, b, (((0,), (0,)), ((), ())), preferred_element_type=F32)


def _mm(a, b):
    return jnp.dot(a, b, preferred_element_type=F32)


def _rms(x, g):
    return x * lax.rsqrt(jnp.mean(x * x, axis=-1, keepdims=True) + EPS) * g


def _params(*sem):
    return pltpu.CompilerParams(dimension_semantics=sem, vmem_limit_bytes=VMEM_LIMIT)


def _const_spec(shape):
    nd = len(shape)
    return pl.BlockSpec(shape, lambda *_: (0,) * nd)


def _in_proj_kernel(x_ref, g_ref, w_ref, aw_ref, ab_ref,
                    qa_ref, ka_ref, va_ref, qg_ref, kg_ref, vg_ref, rg_ref, gate_ref, la_ref):
    h = _rms(x_ref[...], g_ref[...]).astype(BF16)

    def proj(lo, hi):
        return _mm(h, w_ref[:, lo:hi])

    qa_ref[...] = proj(_C_QA, _C_KA).astype(BF16)
    ka_ref[...] = proj(_C_KA, _C_VA).astype(BF16)
    va_ref[...] = proj(_C_VA, _C_QG).astype(BF16)
    qg_ref[...] = proj(_C_QG, _C_KG)
    kg_ref[...] = proj(_C_KG, _C_VG)
    vg_ref[...] = proj(_C_VG, _C_RG)
    rg = proj(_C_RG, _C_GL)
    rg_ref[...] = rg * jax.nn.sigmoid(rg)
    gate_ref[...] = jax.nn.sigmoid(proj(_C_GL, _C_LR)).astype(BF16)
    lr = proj(_C_LR, _C_END).astype(BF16)
    z = _mm(lr, aw_ref[...]) + ab_ref[...]
    la_ref[...] = (jnp.minimum(z, 0.0) - jnp.log1p(jnp.exp(-jnp.abs(z)))) * (1.0 / GLA_GATE_TAU)


def _in_proj(x, g, w, aw, ab, tm=256):
    t = x.shape[0]
    row = lambda c: pl.BlockSpec((tm, c), lambda i: (i, 0))
    outs = [(512, BF16)] * 3 + [(512, F32)] * 4 + [(2048, BF16), (512, F32)]
    return pl.pallas_call(
        _in_proj_kernel,
        grid=(t // tm,),
        in_specs=[row(D_MODEL), _const_spec(g.shape), _const_spec(w.shape),
                  _const_spec(aw.shape), _const_spec(ab.shape)],
        out_specs=[row(c) for c, _ in outs],
        out_shape=[jax.ShapeDtypeStruct((t, c), dt) for c, dt in outs],
        compiler_params=_params("parallel"),
        name="in_proj",
    )(x, g, w, aw, ab)


def _attn_kernel(q_ref, k_ref, v_ref, bias_ref, lq_ref, g_ref, o_ref,
                 qb_s, s0_s, s1_s, p0_s, p1_s, a0_s, a1_s, m_s, l_s, acc_s, *, tq, nk, rb, lam_init):
    i = pl.program_id(2)
    q = q_ref[...]
    lane = lax.broadcasted_iota(jnp.int32, q.shape, 1)
    zero = jnp.zeros_like(q)
    qb_s[0:tq, :] = jnp.where(lane < DA_HEAD_DIM, q, zero)
    qb_s[tq:2 * tq, :] = jnp.where(lane >= DA_HEAD_DIM, q, zero)
    m_s[...] = jnp.full(m_s.shape, -jnp.inf, F32)
    l_s[...] = jnp.zeros(l_s.shape, F32)
    acc_s[...] = jnp.zeros(acc_s.shape, F32)
    s_bufs, p_bufs, a_bufs = (s0_s, s1_s), (p0_s, p1_s), (a0_s, a1_s)
    blocks = [slice(r0, r0 + rb) for r0 in range(0, 2 * tq, rb)]

    def scores(t, buf):
        kc = k_ref[pl.ds(pl.multiple_of(t * tq, tq), tq), :]
        bias = bias_ref[jnp.clip(t - i, -2, 2) + 2]
        s = _nt(qb_s[...], kc)
        s_bufs[buf][0:tq, :] = s[0:tq] + bias
        s_bufs[buf][tq:2 * tq, :] = s[tq:2 * tq] + bias

    def softmax(buf):
        for rows in blocks:
            s = s_bufs[buf][rows, :]
            m_old = m_s[rows, :]
            m_new = jnp.maximum(m_old, jnp.max(s, axis=-1, keepdims=True))
            alpha = jnp.exp2(m_old - m_new)
            ps = [jnp.exp2(s[:, c:c + LANES] - m_new) for c in range(0, tq, LANES)]
            psum = ps[0]
            for pp in ps[1:]:
                psum = psum + pp
            l_s[rows, :] = alpha * l_s[rows, :] + jnp.sum(psum, axis=-1, keepdims=True)
            m_s[rows, :] = m_new
            a_bufs[buf][rows, :] = alpha
            p_bufs[buf][rows, :] = jnp.concatenate([pp.astype(BF16) for pp in ps], axis=1)

    def values(t, buf):
        vc = v_ref[pl.ds(pl.multiple_of(t * tq, tq), tq), :]
        acc_s[...] = a_bufs[buf][...] * acc_s[...] + _mm(p_bufs[buf][...], vc)

    scores(0, 0)
    scores(1, 1)
    softmax(0)

    def steady(u, carry):
        t = 2 + 2 * u
        scores(t, 0)
        softmax(1)
        values(t - 2, 0)
        scores(t + 1, 1)
        softmax(0)
        values(t - 1, 1)
        return carry

    lax.fori_loop(0, (nk - 2) // 2, steady, 0)
    softmax(1)
    values(nk - 2, 0)
    values(nk - 1, 1)

    lq = lq_ref[...]
    lam = (jnp.exp(jnp.sum(lq[0:1] * lq[1:2], axis=-1, keepdims=True))
           - jnp.exp(jnp.sum(lq[2:3] * lq[3:4], axis=-1, keepdims=True)) + lam_init)
    o = acc_s[0:tq, :] / l_s[0:tq, :] - lam * (acc_s[tq:2 * tq, :] / l_s[tq:2 * tq, :])
    o_ref[...] = (_rms(o, g_ref[...]) * (1.0 - lam_init)).astype(BF16)


def _diff_attn(qa, ka, va, bias_tab, lq, g, batch, seq, lam_init):
    tq = ATT_TQ
    nq = seq // tq
    assert nq % 2 == 0 and nq >= 2
    stat = pltpu.VMEM((2 * tq, LANES), F32)
    return pl.pallas_call(
        functools.partial(_attn_kernel, tq=tq, nk=nq, rb=ATT_RB, lam_init=lam_init),
        grid=(batch, DA_HEADS, nq),
        in_specs=[
            pl.BlockSpec((tq, LANES), lambda b, h, i: (b * nq + i, h)),
            pl.BlockSpec((seq, LANES), lambda b, h, i: (b, h)),
            pl.BlockSpec((seq, LANES), lambda b, h, i: (b, h)),
            pl.BlockSpec((None, 5, tq, tq), lambda b, h, i: (h, 0, 0, 0)),
            _const_spec(lq.shape),
            _const_spec(g.shape),
        ],
        out_specs=pl.BlockSpec((tq, LANES), lambda b, h, i: (b * nq + i, h)),
        out_shape=jax.ShapeDtypeStruct(qa.shape, BF16),
        scratch_shapes=[pltpu.VMEM((2 * tq, LANES), BF16),
                        pltpu.VMEM((2 * tq, tq), F32), pltpu.VMEM((2 * tq, tq), F32),
                        pltpu.VMEM((2 * tq, tq), BF16), pltpu.VMEM((2 * tq, tq), BF16),
                        stat, stat, stat, stat,
                        pltpu.VMEM((2 * tq, DA_V_DIM), F32)],
        compiler_params=_params("parallel", "parallel", "arbitrary"),
        name="diff_attn",
    )(qa, ka, va, bias_tab, lq, g)


def _split3(x):
    hi = x.astype(BF16)
    r = x - hi.astype(F32)
    mid = r.astype(BF16)
    lo = (r - mid.astype(F32)).astype(BF16)
    return hi, mid, lo


def _gla_kernel(q_ref, k_ref, v_ref, la_ref, rg_ref, g_ref, o_ref,
                qe_s, kv_s, dec_s, oacc_s, *, nc, rows):
    c = GLA_CHUNK
    gc = GLA_GROUP
    gr = gc * c
    shift = c.bit_length() - 1
    r_i = lax.broadcasted_iota(jnp.int32, (gr, gr), 0)
    c_i = lax.broadcasted_iota(jnp.int32, (gr, gr), 1)
    same_chunk = jnp.right_shift(r_i, shift) == jnp.right_shift(c_i, shift)
    pre = jnp.where(same_chunk & (c_i <= r_i), 1.0, 0.0).astype(BF16)
    suf = jnp.where(same_chunk & (c_i >= r_i), 1.0, 0.0).astype(BF16)
    causal = (lax.broadcasted_iota(jnp.int32, (c, c), 1)
              <= lax.broadcasted_iota(jnp.int32, (c, c), 0))
    fwd_lane_g = lax.broadcasted_iota(jnp.int32, (gr, LANES), 1) < GLA_DK
    fwd_lane = lax.broadcasted_iota(jnp.int32, (c, LANES), 1) < GLA_DK

    def group(gi, _):
        off = pl.multiple_of(gi * gr, gr)
        la = la_ref[pl.ds(off, gr), :]
        b = jnp.zeros((gr, LANES), F32)
        for part in _split3(jnp.where(fwd_lane_g, la, 0.0)):
            b = b + _mm(pre, part)
        for part in _split3(jnp.where(fwd_lane_g, 0.0, la)):
            b = b + _mm(suf, part)
        k = k_ref[pl.ds(off, gr), :]
        qe = q_ref[pl.ds(off, gr), :] * jnp.exp(b)
        qe_s[pl.ds(off, gr), :] = qe
        qeb = qe.astype(BF16)
        keb = (k * jnp.exp(-b)).astype(BF16)
        vb = v_ref[pl.ds(off, gr), :].astype(BF16)
        zero = jnp.zeros((c, LANES), BF16)
        for ci in range(gc):
            rs = slice(ci * c, (ci + 1) * c)
            b_last = jnp.where(fwd_lane[0:1], b[rs][c - 1:c], b[rs][0:1])
            kdb = (k[rs] * jnp.exp(b_last - b[rs])).astype(BF16)
            stacked = jnp.concatenate([jnp.where(fwd_lane, qeb[rs], zero),
                                       jnp.where(fwd_lane, zero, qeb[rs])], axis=0)
            a2 = _nt(stacked, keb[rs])
            att = jnp.where(causal, a2[0:c], a2[c:2 * c])
            oacc_s[pl.ds(off + ci * c, c), :] = _mm(att.astype(BF16), vb[rs])
            n = gi * gc + ci
            kv_s[n] = _tn(vb[rs], kdb)
            dec_s[n] = jnp.broadcast_to(jnp.exp(b_last), (8, LANES))
        return 0

    lax.fori_loop(0, nc // gc, group, 0, unroll=GLA_UNROLL // gc)

    def sweep(forward):
        def step(t, state):
            n = t if forward else nc - 1 - t
            off = pl.multiple_of(n * c, c)
            qe = qe_s[pl.ds(off, c), :]
            qm = jnp.where(fwd_lane, qe, 0.0) if forward else jnp.where(fwd_lane, 0.0, qe)
            oacc_s[pl.ds(off, c), :] += _nt(qm.astype(BF16), state.astype(BF16))
            return state * dec_s[n][0:1] + kv_s[n]
        lax.fori_loop(0, nc, step, jnp.zeros((GLA_DV, LANES), F32), unroll=GLA_UNROLL)

    sweep(True)
    sweep(False)

    def fin(r, _):
        off = pl.multiple_of(r * rows, rows)
        o = oacc_s[pl.ds(off, rows), :]
        o_ref[pl.ds(off, rows), :] = (_rms(o, g_ref[...]) * rg_ref[pl.ds(off, rows), :]).astype(BF16)
        return 0

    lax.fori_loop(0, (nc * c) // rows, fin, 0)


def _gla(qg, kg, vg, la, rgs, g, batch, seq):
    nc = seq // GLA_CHUNK
    blk = pl.BlockSpec((seq, LANES), lambda b, h: (b, h))
    return pl.pallas_call(
        functools.partial(_gla_kernel, nc=nc, rows=256),
        grid=(batch, GLA_HEADS),
        in_specs=[blk, blk, blk, blk, blk, _const_spec(g.shape)],
        out_specs=blk,
        out_shape=jax.ShapeDtypeStruct(vg.shape, BF16),
        scratch_shapes=[
            pltpu.VMEM((seq, LANES), F32),
            pltpu.VMEM((nc, GLA_DV, LANES), F32),
            pltpu.VMEM((nc, 8, LANES), F32),
            pltpu.VMEM((seq, LANES), F32),
        ],
        compiler_params=_params("parallel", "parallel"),
        name="gla",
    )(qg, kg, vg, la, rgs, g)


def _mix_kernel(oa_ref, og_ref, gate_ref, x_ref, wa_ref, wb_ref, wo_ref, g2_ref, wq_ref, sk_ref,
                x1_ref, h2_ref, st_ref):
    ya = _mm(oa_ref[...], wa_ref[...])
    yb = _mm(og_ref[...], wb_ref[...])
    merged = gate_ref[:, :D_MODEL].astype(F32) * ya + gate_ref[:, D_MODEL:].astype(F32) * yb
    x1 = x_ref[...] + _mm(merged.astype(BF16), wo_ref[...])
    x1_ref[...] = x1
    h2 = _rms(x1, g2_ref[...]).astype(BF16)
    h2_ref[...] = h2
    q = _mm(h2, wq_ref[...]).astype(BF16)
    for hc in range(2 * PEER_HEADS):
        st_ref[hc] = _nt(sk_ref[hc], q[:, hc * LANES:(hc + 1) * LANES])


def _mix(oa, og, gates, x, wa, wb, wo, g2, wq, sk, tm=256):
    t = x.shape[0]
    row = lambda c: pl.BlockSpec((tm, c), lambda i: (i, 0))
    return pl.pallas_call(
        _mix_kernel,
        grid=(t // tm,),
        in_specs=[row(512), row(512), row(2048), row(D_MODEL)]
        + [_const_spec(a.shape) for a in (wa, wb, wo, g2, wq, sk)],
        out_specs=[row(D_MODEL), row(D_MODEL),
                   pl.BlockSpec((2 * PEER_HEADS, PEER_N_KEYS, tm), lambda i: (0, 0, i))],
        out_shape=[jax.ShapeDtypeStruct((t, D_MODEL), F32),
                   jax.ShapeDtypeStruct((t, D_MODEL), BF16),
                   jax.ShapeDtypeStruct((2 * PEER_HEADS, PEER_N_KEYS, t), F32)],
        compiler_params=_params("parallel"),
        name="mix",
    )(oa, og, gates, x, wa, wb, wo, g2, wq, sk)


def _extract_top(cur, count, with_rank=False, tie_safe=True):
    rows = lax.broadcasted_iota(jnp.int32, cur.shape, 0)
    rank = jnp.full(cur.shape, float(count), F32)
    out = []
    for r in range(count):
        m = jnp.max(cur, axis=0, keepdims=True)
        out.append(m)
        hit = cur == m
        if tie_safe:
            hit = rows == jnp.min(jnp.where(hit, rows, cur.shape[0]), axis=0, keepdims=True)
        cur = jnp.where(hit, -jnp.inf, cur)
        if with_rank:
            rank = jnp.where(hit, float(r), rank)
    return (out, rank) if with_rank else out


_PAIRS = [(a, b) for a in range(PEER_TOPK) for b in range(PEER_TOPK) if (a + 1) * (b + 1) <= PEER_TOPK]
_CAND_ROWS = -(-len(_PAIRS) // 8) * 8


_HI16 = 0xFFFF0000


def _bf16_hi_bits(x):
    u = lax.bitcast_convert_type(x, jnp.uint32)
    u = u + jnp.uint32(0x7FFF) + ((u >> 16) & jnp.uint32(1))
    return u & jnp.uint32(_HI16)


def _pack_pair(a, b):
    return _bf16_hi_bits(a) | (_bf16_hi_bits(b) >> 16)


def _sum_pair(w):
    return (lax.bitcast_convert_type(w & jnp.uint32(_HI16), F32)
            + lax.bitcast_convert_type(w << 16, F32))


def _head_tables(s1, s2, cand_s, tie_safe):
    k = PEER_TOPK
    lanes = s1.shape[-1]
    v1 = _extract_top(s1, k, tie_safe=tie_safe)
    v2, rank2 = _extract_top(s2, k, with_rank=True, tie_safe=tie_safe)
    cand_s[...] = jnp.full((_CAND_ROWS, lanes), -jnp.inf, F32)
    for r, (a, b) in enumerate(_PAIRS):
        cand_s[r:r + 1, :] = v1[a] + v2[b]
    cand = cand_s[...]
    top = _extract_top(cand, k, tie_safe=tie_safe)
    thr = top[k - 1]
    z = jnp.zeros_like(thr)
    for tk in top:
        z = z + jnp.exp(tk - top[0])
    count = lambda cond: jnp.sum(jnp.where(cond, 1.0, 0.0), axis=0, keepdims=True)
    if tie_safe:
        n1 = jnp.zeros_like(s1)
        for b in range(k):
            n1 = n1 + jnp.where(s1 + v2[b] >= thr, 1.0, 0.0)
        bad = None
    else:
        n1 = jnp.zeros_like(s1)
        for b in range(4):
            n1 = n1 + jnp.where(s1 + v2[b] >= thr, 1.0, 0.0)
        for a in range(3):
            n_a = jnp.zeros_like(thr)
            for b in range(k // (a + 1)):
                n_a = n_a + jnp.where(v1[a] + v2[b] >= thr, 1.0, 0.0)
            n1 = jnp.where(s1 == v1[a], n_a, n1)
        bad = ((count(s1 >= v1[k - 1]) != k) | (count(s2 >= v2[k - 1]) != k)
               | (count(cand >= thr) != k))
    return (rank2, jnp.exp(s2 - v2[0]), n1, jnp.exp(s1 - v1[0]) / z), bad


def _topk_kernel(st_ref, r2_ref, e2_ref, n1_ref, f1_ref, cand_s):
    refs = (r2_ref, e2_ref, n1_ref, f1_ref)
    for hp in range(PEER_HEADS // 2):
        def tables(tie_safe):
            ta, bad_a = _head_tables(st_ref[4 * hp], st_ref[4 * hp + 1], cand_s, tie_safe)
            tb, bad_b = _head_tables(st_ref[4 * hp + 2], st_ref[4 * hp + 3], cand_s, tie_safe)
            for ref, a, b in zip(refs, ta, tb):
                ref[hp] = _pack_pair(a, b)
            return None if tie_safe else jnp.max(jnp.where(bad_a | bad_b, 1.0, 0.0))

        any_bad = tables(tie_safe=False)

        @pl.when(any_bad > 0.0)
        def _():
            tables(tie_safe=True)


def _peer_topk(st, tt=256):
    t = st.shape[-1]
    tab = pl.BlockSpec((PEER_HEADS // 2, PEER_N_KEYS, tt), lambda i: (0, 0, i))
    out = jax.ShapeDtypeStruct((PEER_HEADS // 2, PEER_N_KEYS, t), jnp.uint32)
    return pl.pallas_call(
        _topk_kernel,
        grid=(t // tt,),
        in_specs=[pl.BlockSpec((2 * PEER_HEADS, PEER_N_KEYS, tt), lambda i: (0, 0, i))],
        out_specs=[tab] * 4,
        out_shape=[out] * 4,
        scratch_shapes=[pltpu.VMEM((_CAND_ROWS, tt), F32)],
        compiler_params=_params("parallel"),
        name="peer_topk",
    )(st)


_GELU_C0 = math.sqrt(2.0 / math.pi)
_GELU_C1 = 0.044715 * _GELU_C0


def _gelu_tanh(x):
    hx = 0.5 * x
    return hx + hx * jnp.tanh(x * (_GELU_C0 + _GELU_C1 * (x * x)))


def _peer_kernel(h2_ref, u_ref, vt_ref, r2_ref, e2_ref, n1_ref, f1_ref, x1_ref, x2_ref,
                 acc_s, ut_s, wt_s, *, ec, lw):
    j = pl.program_id(1)
    tt = h2_ref.shape[0]
    zero = jnp.zeros((), BF16)

    @pl.when(j == 0)
    def _():
        acc_s[...] = jnp.zeros_like(acc_s)

    ut_s[...] = _nt(pltpu.bitcast(u_ref[...], BF16), h2_ref[...])
    for r0 in range(0, ec, PEER_N_KEYS):
        q = r0 // PEER_N_KEYS
        grp = j * (ec // PEER_N_KEYS // 8) + q // 8
        sub = slice(q % 8, q % 8 + 1)
        for c0 in range(0, tt, lw):
            cs = slice(c0, c0 + lw)
            gate = jnp.zeros((2 * PEER_N_KEYS, lw), BF16)
            for hp in range(PEER_HEADS // 2):
                row = lambda ref: pltpu.bitcast(
                    jnp.broadcast_to(ref[hp, grp, sub, cs], (PEER_N_KEYS, lw)), BF16)
                r2 = pltpu.bitcast(r2_ref[hp, :, cs], BF16)
                e2 = pltpu.bitcast(e2_ref[hp, :, cs], BF16)
                gate = gate + jnp.where(r2 < row(n1_ref), e2 * row(f1_ref), zero)
            g = _sum_pair(pltpu.bitcast(gate, jnp.uint32))
            act = _gelu_tanh(ut_s[r0:r0 + PEER_N_KEYS, cs])
            wt_s[r0:r0 + PEER_N_KEYS, cs] = (g * act).astype(BF16)
    acc_s[...] += _mm(pltpu.bitcast(vt_ref[...], BF16), wt_s[...])

    @pl.when(j == pl.num_programs(1) - 1)
    def _():
        x2_ref[...] = x1_ref[...] + acc_s[...].T


def _peer_dense(h2, u, vt, r2, e2, n1, f1, x1):
    t = h2.shape[0]
    tt, ec = PEER_TT, PEER_EC
    n_exp = 2 * u.shape[0]
    tab = pl.BlockSpec((PEER_HEADS // 2, PEER_N_KEYS, tt), lambda i, j: (0, 0, i))
    tab1 = pl.BlockSpec((PEER_HEADS // 2, PEER_N_KEYS // 8, 8, tt), lambda i, j: (0, 0, 0, i))
    n1 = n1.reshape(PEER_HEADS // 2, PEER_N_KEYS // 8, 8, t)
    f1 = f1.reshape(PEER_HEADS // 2, PEER_N_KEYS // 8, 8, t)
    return pl.pallas_call(
        functools.partial(_peer_kernel, ec=ec, lw=PEER_LW),
        grid=(t // tt, n_exp // ec),
        in_specs=[
            pl.BlockSpec((tt, D_MODEL), lambda i, j: (i, 0)),
            pl.BlockSpec((ec // 2, D_MODEL), lambda i, j: (j, 0)),
            pl.BlockSpec((D_MODEL // 2, ec), lambda i, j: (0, j)),
            tab, tab, tab1, tab1,
            pl.BlockSpec((tt, D_MODEL), lambda i, j: (i, 0)),
        ],
        out_specs=pl.BlockSpec((tt, D_MODEL), lambda i, j: (i, 0)),
        out_shape=jax.ShapeDtypeStruct((t, D_MODEL), F32),
        scratch_shapes=[pltpu.VMEM((D_MODEL, tt), F32),
                        pltpu.VMEM((ec, tt), F32),
                        pltpu.VMEM((ec, tt), BF16)],
        compiler_params=_params("parallel", "arbitrary"),
        name="peer_dense",
    )(h2, u, vt, r2, e2, n1, f1, x1)


def _ple_kernel(x_ref, p_ref, g3_ref, wg_ref, wp_ref, gf_ref, y_ref, *, final_norm):
    x = x_ref[...]
    h3 = _rms(x, g3_ref[...]).astype(BF16)
    gate = jax.nn.sigmoid(_mm(h3, wg_ref[...]))
    x3 = x + _mm(p_ref[...].astype(BF16), wp_ref[...]) * gate
    y_ref[...] = _rms(x3, gf_ref[...]) if final_norm else x3


def _ple(x, p, g3, wg, wp, gf, final_norm, tm=512):
    t = x.shape[0]
    row = lambda c: pl.BlockSpec((tm, c), lambda i: (i, 0))
    return pl.pallas_call(
        functools.partial(_ple_kernel, final_norm=final_norm),
        grid=(t // tm,),
        in_specs=[row(D_MODEL), row(PLE_DIM)] + [_const_spec(a.shape) for a in (g3, wg, wp, gf)],
        out_specs=row(D_MODEL),
        out_shape=jax.ShapeDtypeStruct((t, D_MODEL), F32),
        compiler_params=_params("parallel"),
        name="ple",
    )(x, p, g3, wg, wp, gf)


def _t5_bucket(rel):
    nb = REL_BUCKETS // 2
    max_exact = nb // 2
    ret = (rel > 0).astype(jnp.int32) * nb
    n = jnp.abs(rel)
    nf = jnp.maximum(n, 1).astype(jnp.float32)
    large = max_exact + (jnp.log(nf / max_exact) / math.log(REL_MAX_DIST / max_exact)
                         * (nb - max_exact)).astype(jnp.int32)
    large = jnp.minimum(large, nb - 1)
    return ret + jnp.where(n < max_exact, n, large)


def _bias_table(rel_bias, tq):
    e = jnp.arange(5, dtype=jnp.int32)[:, None, None]
    r = jnp.arange(tq, dtype=jnp.int32)[None, :, None]
    c = jnp.arange(tq, dtype=jnp.int32)[None, None, :]
    bucket = _t5_bucket(c - r + (e - 2) * tq)[None]
    rb = rel_bias.astype(F32) * LOG2E
    tab = jnp.zeros((rb.shape[1], 5, tq, tq), F32)
    for k in range(REL_BUCKETS):
        tab = jnp.where(bucket == k, rb[k][:, None, None, None], tab)
    return tab


def _pack_row_pairs(w):
    bits = lax.bitcast_convert_type(w.astype(BF16), jnp.uint16).astype(jnp.uint32)
    bits = bits.reshape(w.shape[0] // 2, 2, w.shape[1])
    return bits[:, 0] | (bits[:, 1] << 16)


def _pack_in_proj(w_in, alpha_w, alpha_b):
    qa, ka, va, qg, kg, vg, rg, lr, gl = jnp.split(
        w_in, [512, 1024, 1536, 1792, 2048, 2560, 3072, 3104], axis=1)
    dup = lambda w: jnp.concatenate([w.reshape(-1, GLA_HEADS, 1, GLA_DK)] * 2, axis=2).reshape(-1, 512)
    lr_pad = jnp.pad(lr, ((0, 0), (0, LANES - lr.shape[1])))
    w = jnp.concatenate([qa * (DA_HEAD_DIM ** -0.5 * LOG2E), ka, va, dup(qg) * GLA_DK ** -0.5, dup(kg),
                         vg, rg, gl, lr_pad], axis=1).astype(BF16)
    aw = jnp.zeros((LANES, GLA_HEADS, 2, GLA_DK), F32)
    for d in range(2):
        aw = aw.at[d * GLA_GATE_RANK:(d + 1) * GLA_GATE_RANK, :, d, :].set(
            alpha_w[d].reshape(GLA_GATE_RANK, GLA_HEADS, GLA_DK))
    ab = alpha_b.reshape(2, GLA_HEADS, GLA_DK).transpose(1, 0, 2).reshape(1, 512)
    return w, aw.reshape(LANES, 512).astype(BF16), ab.astype(F32)


def _trunk(x, p, rel_bias, norm1_g, w_in, lambda_qk, da_norm_g, gla_alpha_w, gla_alpha_b, gla_norm_g,
           w_up_a, w_up_b, w_out, norm2_g, peer_w_q, peer_sub_keys, peer_u, peer_v, norm3_g,
           ple_w, ple_gate_w, final_norm_g, prepared):
    batch, seq, _ = x.shape
    depth = w_in.shape[0]
    xf = x.reshape(batch * seq, D_MODEL)
    row = lambda v: v.reshape(1, -1).astype(F32)
    for i in range(depth):
        w1, aw, ab, bias_tab, u_bf, vt_bf, sk = prepared[i]
        lam_init = 0.8 - 0.6 * math.exp(-0.3 * i)
        qa, ka, va, qg, kg, vg, rgs, gates, la = _in_proj(xf, row(norm1_g[i]), w1, aw, ab)
        oa = _diff_attn(qa, ka, va, bias_tab, lambda_qk[i].astype(F32), row(da_norm_g[i]),
                        batch, seq, lam_init)
        og = _gla(qg, kg, vg, la, rgs, row(gla_norm_g[i]), batch, seq)
        x1, h2, st = _mix(oa, og, gates, xf, w_up_a[i].astype(BF16), w_up_b[i].astype(BF16),
                          w_out[i].astype(BF16), row(norm2_g[i]), peer_w_q[i].astype(BF16), sk)
        r2, e2, n1, f1 = _peer_topk(st)
        x2 = _peer_dense(h2, u_bf, vt_bf, r2, e2, n1, f1, x1)
        xf = _ple(x2, p[i].reshape(batch * seq, PLE_DIM), row(norm3_g[i]), ple_gate_w[i].astype(BF16),
                  ple_w[i].astype(BF16), row(final_norm_g), final_norm=(i == depth - 1))
    return xf.reshape(batch, seq, D_MODEL)


def kernel(x_prompt, x_sample, p_prompt, p_sample, rel_bias, norm1_g, w_in, lambda_qk, da_norm_g,
           gla_alpha_w, gla_alpha_b, gla_norm_g, w_up_a, w_up_b, w_out, norm2_g, peer_w_q,
           peer_sub_keys, peer_u, peer_v, norm3_g, ple_w, ple_gate_w, final_norm_g):
    prepared = []
    for i in range(w_in.shape[0]):
        w1, aw, ab = _pack_in_proj(w_in[i], gla_alpha_w[i], gla_alpha_b[i])
        prepared.append((w1, aw, ab, _bias_table(rel_bias, ATT_TQ), _pack_row_pairs(peer_u[i]),
                         _pack_row_pairs(peer_v[i].T),
                         peer_sub_keys[i].reshape(2 * PEER_HEADS, PEER_N_KEYS, -1).astype(BF16)))
    shared = (rel_bias, norm1_g, w_in, lambda_qk, da_norm_g, gla_alpha_w, gla_alpha_b, gla_norm_g,
              w_up_a, w_up_b, w_out, norm2_g, peer_w_q, peer_sub_keys, peer_u, peer_v, norm3_g,
              ple_w, ple_gate_w, final_norm_g)
    y_prompt = _trunk(x_prompt, p_prompt, *shared, prepared)
    y_sample = _trunk(x_sample, p_sample, *shared, prepared)
    return (y_prompt, y_sample)
```

```python
import functools
import math

import jax
import jax.numpy as jnp
from jax import lax
from jax.experimental import pallas as pl
from jax.experimental.pallas import tpu as pltpu

F32 = jnp.float32
BF16 = jnp.bfloat16

D_MODEL = 1024
DA_HEADS = 4
DA_HEAD_DIM = 64
DA_V_DIM = 128
GLA_HEADS = 4
GLA_DK = 64
GLA_DV = 128
GLA_GATE_RANK = 16
GLA_GATE_TAU = 16.0
GLA_CHUNK = 64
REL_BUCKETS = 32
REL_MAX_DIST = 128
PEER_HEADS = 8
PEER_N_KEYS = 128
PEER_TOPK = 16
PLE_DIM = 256
EPS = 1e-6

LANES = 128
V7X_VMEM_BYTES = 64 * 1024 * 1024
VMEM_LIMIT = V7X_VMEM_BYTES - 8 * 1024 * 1024

_C_QA, _C_KA, _C_QG, _C_KG, _C_VG, _C_RG, _C_GL, _C_LR, _C_END = (
    0, 512, 1024, 1536, 2048, 2560, 3072, 5120, 5248)

GLA_UNROLL = 8
GLA_GROUP = 4
ATT_TQ = 256
LOG2E = math.log2(math.e)
PEER_TT = 512
PEER_EC = 2048
PEER_LW = 128


def _nt(a, b):
    return lax.dot_general(a, b, (((1,), (1,)), ((), ())), preferred_element_type=F32)


def _tn(a, b):
    return lax.dot_general(a, b, (((0,), (0,)), ((), ())), preferred_element_type=F32)


def _mm(a, b):
    return jnp.dot(a, b, preferred_element_type=F32)


def _rms(x, g):
    return x * lax.rsqrt(jnp.mean(x * x, axis=-1, keepdims=True) + EPS) * g


def _params(*sem):
    return pltpu.CompilerParams(dimension_semantics=sem, vmem_limit_bytes=VMEM_LIMIT)


def _const_spec(shape):
    nd = len(shape)
    return pl.BlockSpec(shape, lambda *_: (0,) * nd)


def _in_proj_kernel(x_ref, g_ref, w_ref, wvt_ref, aw_ref, ab_ref,
                    qa_ref, ka_ref, vat_ref, qg_ref, kg_ref, vg_ref, rg_ref, gate_ref, la_ref):
    h = _rms(x_ref[...], g_ref[...]).astype(BF16)

    def proj(lo, hi):
        return _mm(h, w_ref[:, lo:hi])

    qa_ref[...] = proj(_C_QA, _C_KA).astype(BF16)
    ka_ref[...] = proj(_C_KA, _C_QG).astype(BF16)
    vat = _nt(wvt_ref[...], h).astype(BF16)
    for hd in range(DA_HEADS):
        vat_ref[hd, 0] = vat[hd * DA_V_DIM:(hd + 1) * DA_V_DIM, :]
    qg_ref[...] = proj(_C_QG, _C_KG)
    kg_ref[...] = proj(_C_KG, _C_VG)
    vg_ref[...] = proj(_C_VG, _C_RG)
    rg = proj(_C_RG, _C_GL)
    rg_ref[...] = rg * jax.nn.sigmoid(rg)
    gate_ref[...] = jax.nn.sigmoid(proj(_C_GL, _C_LR)).astype(BF16)
    lr = proj(_C_LR, _C_END).astype(BF16)
    z = _mm(lr, aw_ref[...]) + ab_ref[...]
    la_ref[...] = (jnp.minimum(z, 0.0) - jnp.log1p(jnp.exp(-jnp.abs(z)))) * (1.0 / GLA_GATE_TAU)


def _in_proj(x, g, w, wvt, aw, ab):
    t = x.shape[0]
    tm = ATT_TQ
    row = lambda c: pl.BlockSpec((tm, c), lambda i: (i, 0))
    outs = [(512, BF16)] * 2 + [None] + [(512, F32)] * 4 + [(2048, BF16), (512, F32)]
    vat_spec = pl.BlockSpec((DA_HEADS, 1, DA_V_DIM, tm), lambda i: (0, i, 0, 0))
    vat_shape = jax.ShapeDtypeStruct((DA_HEADS, t // tm, DA_V_DIM, tm), BF16)
    return pl.pallas_call(
        _in_proj_kernel,
        grid=(t // tm,),
        in_specs=[row(D_MODEL), _const_spec(g.shape), _const_spec(w.shape), _const_spec(wvt.shape),
                  _const_spec(aw.shape), _const_spec(ab.shape)],
        out_specs=[vat_spec if o is None else row(o[0]) for o in outs],
        out_shape=[vat_shape if o is None else jax.ShapeDtypeStruct((t, o[0]), o[1]) for o in outs],
        compiler_params=_params("parallel"),
        name="in_proj",
    )(x, g, w, wvt, aw, ab)


def _attn_kernel(q_ref, k_ref, vt_ref, bias_ref, lq_ref, g_ref, o_ref,
                 qb_s, s0_s, s1_s, p0_s, p1_s, a0_s, a1_s, m_s, l_s, acc_s, *, tq, nk, lam_init):
    i = pl.program_id(2)
    q = q_ref[...]
    lane = lax.broadcasted_iota(jnp.int32, q.shape, 1)
    zero = jnp.zeros_like(q)
    qb_s[0:tq, :] = jnp.where(lane < DA_HEAD_DIM, q, zero)
    qb_s[tq:2 * tq, :] = jnp.where(lane >= DA_HEAD_DIM, q, zero)
    m_s[...] = jnp.full(m_s.shape, -jnp.inf, F32)
    l_s[...] = jnp.zeros(l_s.shape, F32)
    acc_s[...] = jnp.zeros(acc_s.shape, F32)
    s_bufs, p_bufs, a_bufs = (s0_s, s1_s), (p0_s, p1_s), (a0_s, a1_s)

    def scores(t, buf):
        kc = k_ref[pl.ds(pl.multiple_of(t * tq, tq), tq), :]
        bias = bias_ref[jnp.clip(t - i, -2, 2) + 2]
        s = _nt(kc, qb_s[...])
        s_bufs[buf][:, 0:tq] = s[:, 0:tq] + bias
        s_bufs[buf][:, tq:2 * tq] = s[:, tq:2 * tq] + bias

    def softmax(buf):
        for c0 in range(0, 2 * tq, LANES):
            cs = slice(c0, c0 + LANES)
            s = s_bufs[buf][:, cs]
            m_old = m_s[:, cs]
            m_new = jnp.maximum(m_old, jnp.max(s, axis=0, keepdims=True))
            alpha = jnp.exp2(m_old - m_new)
            p = jnp.exp2(s - m_new)
            l_s[:, cs] = alpha * l_s[:, cs] + jnp.sum(p, axis=0, keepdims=True)
            m_s[:, cs] = m_new
            a_bufs[buf][:, cs] = alpha
            p_bufs[buf][:, cs] = p.astype(BF16)

    def values(t, buf):
        acc_s[...] = a_bufs[buf][...] * acc_s[...] + _mm(vt_ref[t], p_bufs[buf][...])

    scores(0, 0)
    scores(1, 1)
    softmax(0)

    def steady(u, carry):
        t = 2 + 2 * u
        scores(t, 0)
        softmax(1)
        values(t - 2, 0)
        scores(t + 1, 1)
        softmax(0)
        values(t - 1, 1)
        return carry

    lax.fori_loop(0, (nk - 2) // 2, steady, 0)
    softmax(1)
    values(nk - 2, 0)
    values(nk - 1, 1)

    lq = lq_ref[...]
    lam = (jnp.exp(jnp.sum(lq[0:1] * lq[1:2], axis=-1, keepdims=True))
           - jnp.exp(jnp.sum(lq[2:3] * lq[3:4], axis=-1, keepdims=True)) + lam_init)
    ot = acc_s[:, 0:tq] / l_s[:, 0:tq] - lam * (acc_s[:, tq:2 * tq] / l_s[:, tq:2 * tq])
    o_ref[...] = (_rms(ot.T, g_ref[...]) * (1.0 - lam_init)).astype(BF16)


def _diff_attn(qa, ka, vat, bias_tab, lq, g, batch, seq, lam_init):
    tq = ATT_TQ
    nq = seq // tq
    assert nq % 2 == 0 and nq >= 2
    stat = pltpu.VMEM((1, 2 * tq), F32)
    return pl.pallas_call(
        functools.partial(_attn_kernel, tq=tq, nk=nq, lam_init=lam_init),
        grid=(batch, DA_HEADS, nq),
        in_specs=[
            pl.BlockSpec((tq, LANES), lambda b, h, i: (b * nq + i, h)),
            pl.BlockSpec((seq, LANES), lambda b, h, i: (b, h)),
            pl.BlockSpec((None, nq, DA_V_DIM, tq), lambda b, h, i: (h, b, 0, 0)),
            pl.BlockSpec((None, 5, tq, tq), lambda b, h, i: (h, 0, 0, 0)),
            _const_spec(lq.shape),
            _const_spec(g.shape),
        ],
        out_specs=pl.BlockSpec((tq, LANES), lambda b, h, i: (b * nq + i, h)),
        out_shape=jax.ShapeDtypeStruct(qa.shape, BF16),
        scratch_shapes=[pltpu.VMEM((2 * tq, LANES), BF16),
                        pltpu.VMEM((tq, 2 * tq), F32), pltpu.VMEM((tq, 2 * tq), F32),
                        pltpu.VMEM((tq, 2 * tq), BF16), pltpu.VMEM((tq, 2 * tq), BF16),
                        stat, stat, stat, stat,
                        pltpu.VMEM((DA_V_DIM, 2 * tq), F32)],
        compiler_params=_params("parallel", "parallel", "arbitrary"),
        name="diff_attn",
    )(qa, ka, vat, bias_tab, lq, g)


def _split3(x):
    hi = x.astype(BF16)
    r = x - hi.astype(F32)
    mid = r.astype(BF16)
    lo = (r - mid.astype(F32)).astype(BF16)
    return hi, mid, lo


def _gla_kernel(q_ref, k_ref, v_ref, la_ref, rg_ref, g_ref, o_ref,
                qe_s, kv_s, dec_s, oacc_s, *, nc, rows):
    c = GLA_CHUNK
    gc = GLA_GROUP
    gr = gc * c
    shift = c.bit_length() - 1
    r_i = lax.broadcasted_iota(jnp.int32, (gr, gr), 0)
    c_i = lax.broadcasted_iota(jnp.int32, (gr, gr), 1)
    same_chunk = jnp.right_shift(r_i, shift) == jnp.right_shift(c_i, shift)
    pre = jnp.where(same_chunk & (c_i <= r_i), 1.0, 0.0).astype(BF16)
    suf = jnp.where(same_chunk & (c_i >= r_i), 1.0, 0.0).astype(BF16)
    causal = (lax.broadcasted_iota(jnp.int32, (c, c), 1)
              <= lax.broadcasted_iota(jnp.int32, (c, c), 0))
    fwd_lane_g = lax.broadcasted_iota(jnp.int32, (gr, LANES), 1) < GLA_DK
    fwd_lane = lax.broadcasted_iota(jnp.int32, (c, LANES), 1) < GLA_DK

    def group(gi, _):
        off = pl.multiple_of(gi * gr, gr)
        la = la_ref[pl.ds(off, gr), :]
        b = jnp.zeros((gr, LANES), F32)
        for part in _split3(jnp.where(fwd_lane_g, la, 0.0)):
            b = b + _mm(pre, part)
        for part in _split3(jnp.where(fwd_lane_g, 0.0, la)):
            b = b + _mm(suf, part)
        k = k_ref[pl.ds(off, gr), :]
        qe = q_ref[pl.ds(off, gr), :] * jnp.exp(b)
        qe_s[pl.ds(off, gr), :] = qe
        qeb = qe.astype(BF16)
        keb = (k * jnp.exp(-b)).astype(BF16)
        vb = v_ref[pl.ds(off, gr), :].astype(BF16)
        zero = jnp.zeros((c, LANES), BF16)
        for ci in range(gc):
            rs = slice(ci * c, (ci + 1) * c)
            b_last = jnp.where(fwd_lane[0:1], b[rs][c - 1:c], b[rs][0:1])
            kdb = (k[rs] * jnp.exp(b_last - b[rs])).astype(BF16)
            stacked = jnp.concatenate([jnp.where(fwd_lane, qeb[rs], zero),
                                       jnp.where(fwd_lane, zero, qeb[rs])], axis=0)
            a2 = _nt(stacked, keb[rs])
            att = jnp.where(causal, a2[0:c], a2[c:2 * c])
            oacc_s[pl.ds(off + ci * c, c), :] = _mm(att.astype(BF16), vb[rs])
            n = gi * gc + ci
            kv_s[n] = _tn(vb[rs], kdb)
            dec_s[n] = jnp.broadcast_to(jnp.exp(b_last), (8, LANES))
        return 0

    lax.fori_loop(0, nc // gc, group, 0, unroll=GLA_UNROLL // gc)

    def sweep(forward):
        def step(t, state):
            n = t if forward else nc - 1 - t
            off = pl.multiple_of(n * c, c)
            qe = qe_s[pl.ds(off, c), :]
            qm = jnp.where(fwd_lane, qe, 0.0) if forward else jnp.where(fwd_lane, 0.0, qe)
            oacc_s[pl.ds(off, c), :] += _nt(qm.astype(BF16), state.astype(BF16))
            return state * dec_s[n][0:1] + kv_s[n]
        lax.fori_loop(0, nc, step, jnp.zeros((GLA_DV, LANES), F32), unroll=GLA_UNROLL)

    sweep(True)
    sweep(False)

    def fin(r, _):
        off = pl.multiple_of(r * rows, rows)
        o = oacc_s[pl.ds(off, rows), :]
        o_ref[pl.ds(off, rows), :] = (_rms(o, g_ref[...]) * rg_ref[pl.ds(off, rows), :]).astype(BF16)
        return 0

    lax.fori_loop(0, (nc * c) // rows, fin, 0)


def _gla(qg, kg, vg, la, rgs, g, batch, seq):
    nc = seq // GLA_CHUNK
    blk = pl.BlockSpec((seq, LANES), lambda b, h: (b, h))
    return pl.pallas_call(
        functools.partial(_gla_kernel, nc=nc, rows=256),
        grid=(batch, GLA_HEADS),
        in_specs=[blk, blk, blk, blk, blk, _const_spec(g.shape)],
        out_specs=blk,
        out_shape=jax.ShapeDtypeStruct(vg.shape, BF16),
        scratch_shapes=[
            pltpu.VMEM((seq, LANES), F32),
            pltpu.VMEM((nc, GLA_DV, LANES), F32),
            pltpu.VMEM((nc, 8, LANES), F32),
            pltpu.VMEM((seq, LANES), F32),
        ],
        compiler_params=_params("parallel", "parallel"),
        name="gla",
    )(qg, kg, vg, la, rgs, g)


def _mix_kernel(oa_ref, og_ref, gate_ref, x_ref, wa_ref, wb_ref, wo_ref, g2_ref, wq_ref, sk_ref,
                x1_ref, h2_ref, st_ref):
    ya = _mm(oa_ref[...], wa_ref[...])
    yb = _mm(og_ref[...], wb_ref[...])
    merged = gate_ref[:, :D_MODEL].astype(F32) * ya + gate_ref[:, D_MODEL:].astype(F32) * yb
    x1 = x_ref[...] + _mm(merged.astype(BF16), wo_ref[...])
    x1_ref[...] = x1
    h2 = _rms(x1, g2_ref[...]).astype(BF16)
    h2_ref[...] = h2
    q = _mm(h2, wq_ref[...]).astype(BF16)
    for hc in range(2 * PEER_HEADS):
        st_ref[hc] = _nt(sk_ref[hc], q[:, hc * LANES:(hc + 1) * LANES])


def _mix(oa, og, gates, x, wa, wb, wo, g2, wq, sk, tm=256):
    t = x.shape[0]
    row = lambda c: pl.BlockSpec((tm, c), lambda i: (i, 0))
    return pl.pallas_call(
        _mix_kernel,
        grid=(t // tm,),
        in_specs=[row(512), row(512), row(2048), row(D_MODEL)]
        + [_const_spec(a.shape) for a in (wa, wb, wo, g2, wq, sk)],
        out_specs=[row(D_MODEL), row(D_MODEL),
                   pl.BlockSpec((2 * PEER_HEADS, PEER_N_KEYS, tm), lambda i: (0, 0, i))],
        out_shape=[jax.ShapeDtypeStruct((t, D_MODEL), F32),
                   jax.ShapeDtypeStruct((t, D_MODEL), BF16),
                   jax.ShapeDtypeStruct((2 * PEER_HEADS, PEER_N_KEYS, t), F32)],
        compiler_params=_params("parallel"),
        name="mix",
    )(oa, og, gates, x, wa, wb, wo, g2, wq, sk)


def _extract_top(cur, count, with_rank=False, tie_safe=True):
    rows = lax.broadcasted_iota(jnp.int32, cur.shape, 0)
    rank = jnp.full(cur.shape, float(count), F32)
    out = []
    for r in range(count):
        m = jnp.max(cur, axis=0, keepdims=True)
        out.append(m)
        hit = cur == m
        if tie_safe:
            hit = rows == jnp.min(jnp.where(hit, rows, cur.shape[0]), axis=0, keepdims=True)
        cur = jnp.where(hit, -jnp.inf, cur)
        if with_rank:
            rank = jnp.where(hit, float(r), rank)
    return (out, rank) if with_rank else out


_PAIRS = [(a, b) for a in range(PEER_TOPK) for b in range(PEER_TOPK) if (a + 1) * (b + 1) <= PEER_TOPK]
_CAND_ROWS = -(-len(_PAIRS) // 8) * 8


_HI16 = 0xFFFF0000


def _bf16_hi_bits(x):
    u = lax.bitcast_convert_type(x, jnp.uint32)
    u = u + jnp.uint32(0x7FFF) + ((u >> 16) & jnp.uint32(1))
    return u & jnp.uint32(_HI16)


def _pack_pair(a, b):
    return _bf16_hi_bits(a) | (_bf16_hi_bits(b) >> 16)


def _sum_pair(w):
    return (lax.bitcast_convert_type(w & jnp.uint32(_HI16), F32)
            + lax.bitcast_convert_type(w << 16, F32))


def _head_tables(s1, s2, cand_s, tie_safe):
    k = PEER_TOPK
    lanes = s1.shape[-1]
    v1 = _extract_top(s1, k, tie_safe=tie_safe)
    v2, rank2 = _extract_top(s2, k, with_rank=True, tie_safe=tie_safe)
    cand_s[...] = jnp.full((_CAND_ROWS, lanes), -jnp.inf, F32)
    for r, (a, b) in enumerate(_PAIRS):
        cand_s[r:r + 1, :] = v1[a] + v2[b]
    cand = cand_s[...]
    top = _extract_top(cand, k, tie_safe=tie_safe)
    thr = top[k - 1]
    z = jnp.zeros_like(thr)
    for tk in top:
        z = z + jnp.exp(tk - top[0])
    count = lambda cond: jnp.sum(jnp.where(cond, 1.0, 0.0), axis=0, keepdims=True)
    if tie_safe:
        n1 = jnp.zeros_like(s1)
        for b in range(k):
            n1 = n1 + jnp.where(s1 + v2[b] >= thr, 1.0, 0.0)
        bad = None
    else:
        n1 = jnp.zeros_like(s1)
        for b in range(4):
            n1 = n1 + jnp.where(s1 + v2[b] >= thr, 1.0, 0.0)
        for a in range(3):
            n_a = jnp.zeros_like(thr)
            for b in range(k // (a + 1)):
                n_a = n_a + jnp.where(v1[a] + v2[b] >= thr, 1.0, 0.0)
            n1 = jnp.where(s1 == v1[a], n_a, n1)
        bad = ((count(s1 >= v1[k - 1]) != k) | (count(s2 >= v2[k - 1]) != k)
               | (count(cand >= thr) != k))
    return (rank2, jnp.exp(s2 - v2[0]), n1, jnp.exp(s1 - v1[0]) / z), bad


def _topk_kernel(st_ref, r2_ref, e2_ref, n1_ref, f1_ref, cand_s):
    refs = (r2_ref, e2_ref, n1_ref, f1_ref)
    for hp in range(PEER_HEADS // 2):
        def tables(tie_safe):
            ta, bad_a = _head_tables(st_ref[4 * hp], st_ref[4 * hp + 1], cand_s, tie_safe)
            tb, bad_b = _head_tables(st_ref[4 * hp + 2], st_ref[4 * hp + 3], cand_s, tie_safe)
            for ref, a, b in zip(refs, ta, tb):
                ref[hp] = _pack_pair(a, b)
            return None if tie_safe else jnp.max(jnp.where(bad_a | bad_b, 1.0, 0.0))

        any_bad = tables(tie_safe=False)

        @pl.when(any_bad > 0.0)
        def _():
            tables(tie_safe=True)


def _peer_topk(st, tt=256):
    t = st.shape[-1]
    tab = pl.BlockSpec((PEER_HEADS // 2, PEER_N_KEYS, tt), lambda i: (0, 0, i))
    out = jax.ShapeDtypeStruct((PEER_HEADS // 2, PEER_N_KEYS, t), jnp.uint32)
    return pl.pallas_call(
        _topk_kernel,
        grid=(t // tt,),
        in_specs=[pl.BlockSpec((2 * PEER_HEADS, PEER_N_KEYS, tt), lambda i: (0, 0, i))],
        out_specs=[tab] * 4,
        out_shape=[out] * 4,
        scratch_shapes=[pltpu.VMEM((_CAND_ROWS, tt), F32)],
        compiler_params=_params("parallel"),
        name="peer_topk",
    )(st)


_GELU_C0 = math.sqrt(2.0 / math.pi)
_GELU_C1 = 0.044715 * _GELU_C0


def _gelu_tanh(x):
    hx = 0.5 * x
    return hx + hx * jnp.tanh(x * (_GELU_C0 + _GELU_C1 * (x * x)))


def _peer_kernel(h2_ref, u_ref, vt_ref, r2_ref, e2_ref, n1_ref, f1_ref, x1_ref, x2_ref,
                 acc_s, ut_s, wt_s, *, ec, lw):
    j = pl.program_id(1)
    tt = h2_ref.shape[0]
    zero = jnp.zeros((), BF16)

    @pl.when(j == 0)
    def _():
        acc_s[...] = jnp.zeros_like(acc_s)

    ut_s[...] = _nt(pltpu.bitcast(u_ref[...], BF16), h2_ref[...])
    for r0 in range(0, ec, PEER_N_KEYS):
        q = r0 // PEER_N_KEYS
        grp = j * (ec // PEER_N_KEYS // 8) + q // 8
        sub = slice(q % 8, q % 8 + 1)
        for c0 in range(0, tt, lw):
            cs = slice(c0, c0 + lw)
            gate = jnp.zeros((2 * PEER_N_KEYS, lw), BF16)
            for hp in range(PEER_HEADS // 2):
                row = lambda ref: pltpu.bitcast(
                    jnp.broadcast_to(ref[hp, grp, sub, cs], (PEER_N_KEYS, lw)), BF16)
                r2 = pltpu.bitcast(r2_ref[hp, :, cs], BF16)
                e2 = pltpu.bitcast(e2_ref[hp, :, cs], BF16)
                gate = gate + jnp.where(r2 < row(n1_ref), e2 * row(f1_ref), zero)
            g = _sum_pair(pltpu.bitcast(gate, jnp.uint32))
            act = _gelu_tanh(ut_s[r0:r0 + PEER_N_KEYS, cs])
            wt_s[r0:r0 + PEER_N_KEYS, cs] = (g * act).astype(BF16)
    acc_s[...] += _mm(pltpu.bitcast(vt_ref[...], BF16), wt_s[...])

    @pl.when(j == pl.num_programs(1) - 1)
    def _():
        x2_ref[...] = x1_ref[...] + acc_s[...].T


def _peer_dense(h2, u, vt, r2, e2, n1, f1, x1):
    t = h2.shape[0]
    tt, ec = PEER_TT, PEER_EC
    n_exp = 2 * u.shape[0]
    tab = pl.BlockSpec((PEER_HEADS // 2, PEER_N_KEYS, tt), lambda i, j: (0, 0, i))
    tab1 = pl.BlockSpec((PEER_HEADS // 2, PEER_N_KEYS // 8, 8, tt), lambda i, j: (0, 0, 0, i))
    n1 = n1.reshape(PEER_HEADS // 2, PEER_N_KEYS // 8, 8, t)
    f1 = f1.reshape(PEER_HEADS // 2, PEER_N_KEYS // 8, 8, t)
    return pl.pallas_call(
        functools.partial(_peer_kernel, ec=ec, lw=PEER_LW),
        grid=(t // tt, n_exp // ec),
        in_specs=[
            pl.BlockSpec((tt, D_MODEL), lambda i, j: (i, 0)),
            pl.BlockSpec((ec // 2, D_MODEL), lambda i, j: (j, 0)),
            pl.BlockSpec((D_MODEL // 2, ec), lambda i, j: (0, j)),
            tab, tab, tab1, tab1,
            pl.BlockSpec((tt, D_MODEL), lambda i, j: (i, 0)),
        ],
        out_specs=pl.BlockSpec((tt, D_MODEL), lambda i, j: (i, 0)),
        out_shape=jax.ShapeDtypeStruct((t, D_MODEL), F32),
        scratch_shapes=[pltpu.VMEM((D_MODEL, tt), F32),
                        pltpu.VMEM((ec, tt), F32),
                        pltpu.VMEM((ec, tt), BF16)],
        compiler_params=_params("parallel", "arbitrary"),
        name="peer_dense",
    )(h2, u, vt, r2, e2, n1, f1, x1)


def _ple_kernel(x_ref, p_ref, g3_ref, wg_ref, wp_ref, gf_ref, y_ref, *, final_norm):
    x = x_ref[...]
    h3 = _rms(x, g3_ref[...]).astype(BF16)
    gate = jax.nn.sigmoid(_mm(h3, wg_ref[...]))
    x3 = x + _mm(p_ref[...].astype(BF16), wp_ref[...]) * gate
    y_ref[...] = _rms(x3, gf_ref[...]) if final_norm else x3


def _ple(x, p, g3, wg, wp, gf, final_norm, tm=512):
    t = x.shape[0]
    row = lambda c: pl.BlockSpec((tm, c), lambda i: (i, 0))
    return pl.pallas_call(
        functools.partial(_ple_kernel, final_norm=final_norm),
        grid=(t // tm,),
        in_specs=[row(D_MODEL), row(PLE_DIM)] + [_const_spec(a.shape) for a in (g3, wg, wp, gf)],
        out_specs=row(D_MODEL),
        out_shape=jax.ShapeDtypeStruct((t, D_MODEL), F32),
        compiler_params=_params("parallel"),
        name="ple",
    )(x, p, g3, wg, wp, gf)


def _t5_bucket(rel):
    nb = REL_BUCKETS // 2
    max_exact = nb // 2
    ret = (rel > 0).astype(jnp.int32) * nb
    n = jnp.abs(rel)
    nf = jnp.maximum(n, 1).astype(jnp.float32)
    large = max_exact + (jnp.log(nf / max_exact) / math.log(REL_MAX_DIST / max_exact)
                         * (nb - max_exact)).astype(jnp.int32)
    large = jnp.minimum(large, nb - 1)
    return ret + jnp.where(n < max_exact, n, large)


def _bias_table(rel_bias, tq):
    e = jnp.arange(5, dtype=jnp.int32)[:, None, None]
    r = jnp.arange(tq, dtype=jnp.int32)[None, :, None]
    c = jnp.arange(tq, dtype=jnp.int32)[None, None, :]
    bucket = _t5_bucket(r - c + (e - 2) * tq)[None]
    rb = rel_bias.astype(F32) * LOG2E
    tab = jnp.zeros((rb.shape[1], 5, tq, tq), F32)
    for k in range(REL_BUCKETS):
        tab = jnp.where(bucket == k, rb[k][:, None, None, None], tab)
    return tab


def _pack_row_pairs(w):
    bits = lax.bitcast_convert_type(w.astype(BF16), jnp.uint16).astype(jnp.uint32)
    bits = bits.reshape(w.shape[0] // 2, 2, w.shape[1])
    return bits[:, 0] | (bits[:, 1] << 16)


def _pack_in_proj(w_in, alpha_w, alpha_b):
    qa, ka, va, qg, kg, vg, rg, lr, gl = jnp.split(
        w_in, [512, 1024, 1536, 1792, 2048, 2560, 3072, 3104], axis=1)
    dup = lambda w: jnp.concatenate([w.reshape(-1, GLA_HEADS, 1, GLA_DK)] * 2, axis=2).reshape(-1, 512)
    lr_pad = jnp.pad(lr, ((0, 0), (0, LANES - lr.shape[1])))
    w = jnp.concatenate([qa * (DA_HEAD_DIM ** -0.5 * LOG2E), ka, dup(qg) * GLA_DK ** -0.5, dup(kg),
                         vg, rg, gl, lr_pad], axis=1).astype(BF16)
    aw = jnp.zeros((LANES, GLA_HEADS, 2, GLA_DK), F32)
    for d in range(2):
        aw = aw.at[d * GLA_GATE_RANK:(d + 1) * GLA_GATE_RANK, :, d, :].set(
            alpha_w[d].reshape(GLA_GATE_RANK, GLA_HEADS, GLA_DK))
    ab = alpha_b.reshape(2, GLA_HEADS, GLA_DK).transpose(1, 0, 2).reshape(1, 512)
    return w, va.T.astype(BF16), aw.reshape(LANES, 512).astype(BF16), ab.astype(F32)


def _trunk(x, p, rel_bias, norm1_g, w_in, lambda_qk, da_norm_g, gla_alpha_w, gla_alpha_b, gla_norm_g,
           w_up_a, w_up_b, w_out, norm2_g, peer_w_q, peer_sub_keys, peer_u, peer_v, norm3_g,
           ple_w, ple_gate_w, final_norm_g, prepared):
    batch, seq, _ = x.shape
    depth = w_in.shape[0]
    xf = x.reshape(batch * seq, D_MODEL)
    row = lambda v: v.reshape(1, -1).astype(F32)
    for i in range(depth):
        w1, wvt, aw, ab, bias_tab, u_bf, vt_bf, sk = prepared[i]
        lam_init = 0.8 - 0.6 * math.exp(-0.3 * i)
        qa, ka, vat, qg, kg, vg, rgs, gates, la = _in_proj(xf, row(norm1_g[i]), w1, wvt, aw, ab)
        oa = _diff_attn(qa, ka, vat, bias_tab, lambda_qk[i].astype(F32), row(da_norm_g[i]),
                        batch, seq, lam_init)
        og = _gla(qg, kg, vg, la, rgs, row(gla_norm_g[i]), batch, seq)
        x1, h2, st = _mix(oa, og, gates, xf, w_up_a[i].astype(BF16), w_up_b[i].astype(BF16),
                          w_out[i].astype(BF16), row(norm2_g[i]), peer_w_q[i].astype(BF16), sk)
        r2, e2, n1, f1 = _peer_topk(st)
        x2 = _peer_dense(h2, u_bf, vt_bf, r2, e2, n1, f1, x1)
        xf = _ple(x2, p[i].reshape(batch * seq, PLE_DIM), row(norm3_g[i]), ple_gate_w[i].astype(BF16),
                  ple_w[i].astype(BF16), row(final_norm_g), final_norm=(i == depth - 1))
    return xf.reshape(batch, seq, D_MODEL)


def kernel(x_prompt, x_sample, p_prompt, p_sample, rel_bias, norm1_g, w_in, lambda_qk, da_norm_g,
           gla_alpha_w, gla_alpha_b, gla_norm_g, w_up_a, w_up_b, w_out, norm2_g, peer_w_q,
           peer_sub_keys, peer_u, peer_v, norm3_g, ple_w, ple_gate_w, final_norm_g):
    prepared = []
    for i in range(w_in.shape[0]):
        w1, wvt, aw, ab = _pack_in_proj(w_in[i], gla_alpha_w[i], gla_alpha_b[i])
        prepared.append((w1, wvt, aw, ab, _bias_table(rel_bias, ATT_TQ), _pack_row_pairs(peer_u[i]),
                         _pack_row_pairs(peer_v[i].T),
                         peer_sub_keys[i].reshape(2 * PEER_HEADS, PEER_N_KEYS, -1).astype(BF16)))
    shared = (rel_bias, norm1_g, w_in, lambda_qk, da_norm_g, gla_alpha_w, gla_alpha_b, gla_norm_g,
              w_up_a, w_up_b, w_out, norm2_g, peer_w_q, peer_sub_keys, peer_u, peer_v, norm3_g,
              ple_w, ple_gate_w, final_norm_g)
    y_prompt = _trunk(x_prompt, p_prompt, *shared, prepared)
    y_sample = _trunk(x_sample, p_sample, *shared, prepared)
    return (y_prompt, y_sample)
```

```python
import functools
import math

import jax
import jax.numpy as jnp
from jax import lax
from jax.experimental import pallas as pl
from jax.experimental.pallas import tpu as pltpu

F32 = jnp.float32
BF16 = jnp.bfloat16

D_MODEL = 1024
DA_HEADS = 4
DA_HEAD_DIM = 64
DA_V_DIM = 128
GLA_HEADS = 4
GLA_DK = 64
GLA_DV = 128
GLA_GATE_RANK = 16
GLA_GATE_TAU = 16.0
GLA_CHUNK = 64
REL_BUCKETS = 32
REL_MAX_DIST = 128
PEER_HEADS = 8
PEER_N_KEYS = 128
PEER_TOPK = 16
PLE_DIM = 256
EPS = 1e-6

LANES = 128
V7X_VMEM_BYTES = 64 * 1024 * 1024
VMEM_LIMIT = V7X_VMEM_BYTES - 8 * 1024 * 1024

_C_QA, _C_KA, _C_QG, _C_KG, _C_VG, _C_RG, _C_GL, _C_LR, _C_END = (
    0, 512, 1024, 1536, 2048, 2560, 3072, 5120, 5248)

GLA_UNROLL = 8
GLA_GROUP = 4
ATT_TQ = 256
IN_PROJ_TM = 512
LOG2E = math.log2(math.e)
PEER_TT = 512
PEER_EC = 2048
PEER_LW = 128


def _nt(a, b):
    return lax.dot_general(a, b, (((1,), (1,)), ((), ())), preferred_element_type=F32)


def _tn(a, b):
    return lax.dot_general(a, b, (((0,), (0,)), ((), ())), preferred_element_type=F32)


def _mm(a, b):
    return jnp.dot(a, b, preferred_element_type=F32)


def _rms(x, g):
    return x * lax.rsqrt(jnp.mean(x * x, axis=-1, keepdims=True) + EPS) * g


def _params(*sem):
    return pltpu.CompilerParams(dimension_semantics=sem, vmem_limit_bytes=VMEM_LIMIT)


def _const_spec(shape):
    nd = len(shape)
    return pl.BlockSpec(shape, lambda *_: (0,) * nd)


def _in_proj_kernel(x_ref, g_ref, w_ref, wvt_ref, aw_ref, ab_ref,
                    qa_ref, ka_ref, vat_ref, qg_ref, kg_ref, vg_ref, rg_ref, gate_ref, la_ref):
    h = _rms(x_ref[...], g_ref[...]).astype(BF16)

    def proj(lo, hi):
        return _mm(h, w_ref[:, lo:hi])

    qa_ref[...] = proj(_C_QA, _C_KA).astype(BF16)
    ka_ref[...] = proj(_C_KA, _C_QG).astype(BF16)
    vat = _nt(wvt_ref[...], h).astype(BF16)
    for hd in range(DA_HEADS):
        for c in range(vat_ref.shape[1]):
            vat_ref[hd, c] = vat[hd * DA_V_DIM:(hd + 1) * DA_V_DIM, c * ATT_TQ:(c + 1) * ATT_TQ]
    qg_ref[...] = proj(_C_QG, _C_KG)
    kg_ref[...] = proj(_C_KG, _C_VG)
    vg_ref[...] = proj(_C_VG, _C_RG)
    rg = proj(_C_RG, _C_GL)
    rg_ref[...] = rg * jax.nn.sigmoid(rg)
    gate_ref[...] = jax.nn.sigmoid(proj(_C_GL, _C_LR)).astype(BF16)
    lr = proj(_C_LR, _C_END).astype(BF16)
    z = _mm(lr, aw_ref[...]) + ab_ref[...]
    la_ref[...] = (jnp.minimum(z, 0.0) - jnp.log1p(jnp.exp(-jnp.abs(z)))) * (1.0 / GLA_GATE_TAU)


def _in_proj(x, g, w, wvt, aw, ab):
    t = x.shape[0]
    tm = IN_PROJ_TM
    kc = tm // ATT_TQ
    row = lambda c: pl.BlockSpec((tm, c), lambda i: (i, 0))
    outs = [(512, BF16)] * 2 + [None] + [(512, F32)] * 4 + [(2048, BF16), (512, F32)]
    vat_spec = pl.BlockSpec((DA_HEADS, kc, DA_V_DIM, ATT_TQ), lambda i: (0, i, 0, 0))
    vat_shape = jax.ShapeDtypeStruct((DA_HEADS, t // ATT_TQ, DA_V_DIM, ATT_TQ), BF16)
    return pl.pallas_call(
        _in_proj_kernel,
        grid=(t // tm,),
        in_specs=[row(D_MODEL), _const_spec(g.shape), _const_spec(w.shape), _const_spec(wvt.shape),
                  _const_spec(aw.shape), _const_spec(ab.shape)],
        out_specs=[vat_spec if o is None else row(o[0]) for o in outs],
        out_shape=[vat_shape if o is None else jax.ShapeDtypeStruct((t, o[0]), o[1]) for o in outs],
        compiler_params=_params("parallel"),
        name="in_proj",
    )(x, g, w, wvt, aw, ab)


def _attn_kernel(q_ref, k_ref, vt_ref, bias_ref, lq_ref, g_ref, o_ref,
                 qb_s, s0_s, s1_s, p0_s, p1_s, a0_s, a1_s, m_s, l_s, acc_s, *, tq, nk, lam_init):
    i = pl.program_id(2)
    q = q_ref[...]
    lane = lax.broadcasted_iota(jnp.int32, q.shape, 1)
    zero = jnp.zeros_like(q)
    qb_s[0:tq, :] = jnp.where(lane < DA_HEAD_DIM, q, zero)
    qb_s[tq:2 * tq, :] = jnp.where(lane >= DA_HEAD_DIM, q, zero)
    m_s[...] = jnp.full(m_s.shape, -jnp.inf, F32)
    l_s[...] = jnp.zeros(l_s.shape, F32)
    acc_s[...] = jnp.zeros(acc_s.shape, F32)
    s_bufs, p_bufs, a_bufs = (s0_s, s1_s), (p0_s, p1_s), (a0_s, a1_s)

    def scores(t, buf):
        kc = k_ref[pl.ds(pl.multiple_of(t * tq, tq), tq), :]
        bias = bias_ref[jnp.clip(t - i, -2, 2) + 2]
        s = _nt(kc, qb_s[...])
        s_bufs[buf][:, 0:tq] = s[:, 0:tq] + bias
        s_bufs[buf][:, tq:2 * tq] = s[:, tq:2 * tq] + bias

    def softmax(buf):
        for c0 in range(0, 2 * tq, LANES):
            cs = slice(c0, c0 + LANES)
            s = s_bufs[buf][:, cs]
            m_old = m_s[:, cs]
            m_new = jnp.maximum(m_old, jnp.max(s, axis=0, keepdims=True))
            alpha = jnp.exp2(m_old - m_new)
            p = jnp.exp2(s - m_new)
            l_s[:, cs] = alpha * l_s[:, cs] + jnp.sum(p, axis=0, keepdims=True)
            m_s[:, cs] = m_new
            a_bufs[buf][:, cs] = alpha
            p_bufs[buf][:, cs] = p.astype(BF16)

    def values(t, buf):
        acc_s[...] = a_bufs[buf][...] * acc_s[...] + _mm(vt_ref[t], p_bufs[buf][...])

    scores(0, 0)
    scores(1, 1)
    softmax(0)

    def steady(u, carry):
        t = 2 + 2 * u
        scores(t, 0)
        softmax(1)
        values(t - 2, 0)
        scores(t + 1, 1)
        softmax(0)
        values(t - 1, 1)
        return carry

    lax.fori_loop(0, (nk - 2) // 2, steady, 0)
    softmax(1)
    values(nk - 2, 0)
    values(nk - 1, 1)

    lq = lq_ref[...]
    lam = (jnp.exp(jnp.sum(lq[0:1] * lq[1:2], axis=-1, keepdims=True))
           - jnp.exp(jnp.sum(lq[2:3] * lq[3:4], axis=-1, keepdims=True)) + lam_init)
    ot = acc_s[:, 0:tq] / l_s[:, 0:tq] - lam * (acc_s[:, tq:2 * tq] / l_s[:, tq:2 * tq])
    o_ref[...] = (_rms(ot.T, g_ref[...]) * (1.0 - lam_init)).astype(BF16)


def _diff_attn(qa, ka, vat, bias_tab, lq, g, batch, seq, lam_init):
    tq = ATT_TQ
    nq = seq // tq
    assert nq % 2 == 0 and nq >= 2
    stat = pltpu.VMEM((1, 2 * tq), F32)
    return pl.pallas_call(
        functools.partial(_attn_kernel, tq=tq, nk=nq, lam_init=lam_init),
        grid=(batch, DA_HEADS, nq),
        in_specs=[
            pl.BlockSpec((tq, LANES), lambda b, h, i: (b * nq + i, h)),
            pl.BlockSpec((seq, LANES), lambda b, h, i: (b, h)),
            pl.BlockSpec((None, nq, DA_V_DIM, tq), lambda b, h, i: (h, b, 0, 0)),
            pl.BlockSpec((None, 5, tq, tq), lambda b, h, i: (h, 0, 0, 0)),
            _const_spec(lq.shape),
            _const_spec(g.shape),
        ],
        out_specs=pl.BlockSpec((tq, LANES), lambda b, h, i: (b * nq + i, h)),
        out_shape=jax.ShapeDtypeStruct(qa.shape, BF16),
        scratch_shapes=[pltpu.VMEM((2 * tq, LANES), BF16),
                        pltpu.VMEM((tq, 2 * tq), F32), pltpu.VMEM((tq, 2 * tq), F32),
                        pltpu.VMEM((tq, 2 * tq), BF16), pltpu.VMEM((tq, 2 * tq), BF16),
                        stat, stat, stat, stat,
                        pltpu.VMEM((DA_V_DIM, 2 * tq), F32)],
        compiler_params=_params("parallel", "parallel", "arbitrary"),
        name="diff_attn",
    )(qa, ka, vat, bias_tab, lq, g)


def _split3(x):
    hi = x.astype(BF16)
    r = x - hi.astype(F32)
    mid = r.astype(BF16)
    lo = (r - mid.astype(F32)).astype(BF16)
    return hi, mid, lo


def _gla_kernel(q_ref, k_ref, v_ref, la_ref, rg_ref, g_ref, o_ref,
                qe_s, kv_s, dec_s, oacc_s, *, nc, rows):
    c = GLA_CHUNK
    gc = GLA_GROUP
    gr = gc * c
    shift = c.bit_length() - 1
    r_i = lax.broadcasted_iota(jnp.int32, (gr, gr), 0)
    c_i = lax.broadcasted_iota(jnp.int32, (gr, gr), 1)
    same_chunk = jnp.right_shift(r_i, shift) == jnp.right_shift(c_i, shift)
    pre = jnp.where(same_chunk & (c_i <= r_i), 1.0, 0.0).astype(BF16)
    suf = jnp.where(same_chunk & (c_i >= r_i), 1.0, 0.0).astype(BF16)
    causal = (lax.broadcasted_iota(jnp.int32, (c, c), 1)
              <= lax.broadcasted_iota(jnp.int32, (c, c), 0))
    fwd_lane_g = lax.broadcasted_iota(jnp.int32, (gr, LANES), 1) < GLA_DK
    fwd_lane = lax.broadcasted_iota(jnp.int32, (c, LANES), 1) < GLA_DK

    def group(gi, _):
        off = pl.multiple_of(gi * gr, gr)
        la = la_ref[pl.ds(off, gr), :]
        b = jnp.zeros((gr, LANES), F32)
        for part in _split3(jnp.where(fwd_lane_g, la, 0.0)):
            b = b + _mm(pre, part)
        for part in _split3(jnp.where(fwd_lane_g, 0.0, la)):
            b = b + _mm(suf, part)
        k = k_ref[pl.ds(off, gr), :]
        qe = q_ref[pl.ds(off, gr), :] * jnp.exp(b)
        qe_s[pl.ds(off, gr), :] = qe
        qeb = qe.astype(BF16)
        keb = (k * jnp.exp(-b)).astype(BF16)
        vb = v_ref[pl.ds(off, gr), :].astype(BF16)
        zero = jnp.zeros((c, LANES), BF16)
        for ci in range(gc):
            rs = slice(ci * c, (ci + 1) * c)
            b_last = jnp.where(fwd_lane[0:1], b[rs][c - 1:c], b[rs][0:1])
            kdb = (k[rs] * jnp.exp(b_last - b[rs])).astype(BF16)
            stacked = jnp.concatenate([jnp.where(fwd_lane, qeb[rs], zero),
                                       jnp.where(fwd_lane, zero, qeb[rs])], axis=0)
            a2 = _nt(stacked, keb[rs])
            att = jnp.where(causal, a2[0:c], a2[c:2 * c])
            oacc_s[pl.ds(off + ci * c, c), :] = _mm(att.astype(BF16), vb[rs])
            n = gi * gc + ci
            kv_s[n] = _tn(vb[rs], kdb)
            dec_s[n] = jnp.broadcast_to(jnp.exp(b_last), (8, LANES))
        return 0

    lax.fori_loop(0, nc // gc, group, 0, unroll=GLA_UNROLL // gc)

    def sweep(forward):
        def step(t, state):
            n = t if forward else nc - 1 - t
            off = pl.multiple_of(n * c, c)
            qe = qe_s[pl.ds(off, c), :]
            qm = jnp.where(fwd_lane, qe, 0.0) if forward else jnp.where(fwd_lane, 0.0, qe)
            oacc_s[pl.ds(off, c), :] += _nt(qm.astype(BF16), state.astype(BF16))
            return state * dec_s[n][0:1] + kv_s[n]
        lax.fori_loop(0, nc, step, jnp.zeros((GLA_DV, LANES), F32), unroll=GLA_UNROLL)

    sweep(True)
    sweep(False)

    def fin(r, _):
        off = pl.multiple_of(r * rows, rows)
        o = oacc_s[pl.ds(off, rows), :]
        o_ref[pl.ds(off, rows), :] = (_rms(o, g_ref[...]) * rg_ref[pl.ds(off, rows), :]).astype(BF16)
        return 0

    lax.fori_loop(0, (nc * c) // rows, fin, 0)


def _gla(qg, kg, vg, la, rgs, g, batch, seq):
    nc = seq // GLA_CHUNK
    blk = pl.BlockSpec((seq, LANES), lambda b, h: (b, h))
    return pl.pallas_call(
        functools.partial(_gla_kernel, nc=nc, rows=256),
        grid=(batch, GLA_HEADS),
        in_specs=[blk, blk, blk, blk, blk, _const_spec(g.shape)],
        out_specs=blk,
        out_shape=jax.ShapeDtypeStruct(vg.shape, BF16),
        scratch_shapes=[
            pltpu.VMEM((seq, LANES), F32),
            pltpu.VMEM((nc, GLA_DV, LANES), F32),
            pltpu.VMEM((nc, 8, LANES), F32),
            pltpu.VMEM((seq, LANES), F32),
        ],
        compiler_params=_params("parallel", "parallel"),
        name="gla",
    )(qg, kg, vg, la, rgs, g)


def _mix_kernel(oa_ref, og_ref, gate_ref, x_ref, wa_ref, wb_ref, wo_ref, g2_ref, wq_ref, sk_ref,
                x1_ref, h2_ref, st_ref):
    ya = _mm(oa_ref[...], wa_ref[...])
    yb = _mm(og_ref[...], wb_ref[...])
    merged = gate_ref[:, :D_MODEL].astype(F32) * ya + gate_ref[:, D_MODEL:].astype(F32) * yb
    x1 = x_ref[...] + _mm(merged.astype(BF16), wo_ref[...])
    x1_ref[...] = x1
    h2 = _rms(x1, g2_ref[...]).astype(BF16)
    h2_ref[...] = h2
    q = _mm(h2, wq_ref[...]).astype(BF16)
    for hc in range(2 * PEER_HEADS):
        st_ref[hc] = _nt(sk_ref[hc], q[:, hc * LANES:(hc + 1) * LANES])


def _mix(oa, og, gates, x, wa, wb, wo, g2, wq, sk, tm=512):
    t = x.shape[0]
    row = lambda c: pl.BlockSpec((tm, c), lambda i: (i, 0))
    return pl.pallas_call(
        _mix_kernel,
        grid=(t // tm,),
        in_specs=[row(512), row(512), row(2048), row(D_MODEL)]
        + [_const_spec(a.shape) for a in (wa, wb, wo, g2, wq, sk)],
        out_specs=[row(D_MODEL), row(D_MODEL),
                   pl.BlockSpec((2 * PEER_HEADS, PEER_N_KEYS, tm), lambda i: (0, 0, i))],
        out_shape=[jax.ShapeDtypeStruct((t, D_MODEL), F32),
                   jax.ShapeDtypeStruct((t, D_MODEL), BF16),
                   jax.ShapeDtypeStruct((2 * PEER_HEADS, PEER_N_KEYS, t), F32)],
        compiler_params=_params("parallel"),
        name="mix",
    )(oa, og, gates, x, wa, wb, wo, g2, wq, sk)


def _extract_top(cur, count, with_rank=False, tie_safe=True):
    rows = lax.broadcasted_iota(jnp.int32, cur.shape, 0)
    rank = jnp.full(cur.shape, float(count), F32)
    out = []
    for r in range(count):
        m = jnp.max(cur, axis=0, keepdims=True)
        out.append(m)
        hit = cur == m
        if tie_safe:
            hit = rows == jnp.min(jnp.where(hit, rows, cur.shape[0]), axis=0, keepdims=True)
        cur = jnp.where(hit, -jnp.inf, cur)
        if with_rank:
            rank = jnp.where(hit, float(r), rank)
    return (out, rank) if with_rank else out


def _oddeven_merge_sort_pairs(n):
    pairs = []
    p = 1
    while p < n:
        k = p
        while k >= 1:
            for j in range(k % p, n - k, 2 * k):
                for i in range(min(k, n - j - k)):
                    if (i + j) // (2 * p) == (i + j + k) // (2 * p):
                        pairs.append((i + j, i + j + k))
            k //= 2
        p *= 2
    return pairs


_SORT16 = _oddeven_merge_sort_pairs(PEER_N_KEYS // 8)


def _top_distinct(x, count, with_rank=False):
    g = x.shape[0] // 8
    lev = [x[8 * i:8 * (i + 1)] for i in range(g)]
    for i, j in _SORT16:
        lev[i], lev[j] = jnp.maximum(lev[i], lev[j]), jnp.minimum(lev[i], lev[j])
    out = []
    for r in range(count):
        m = jnp.max(lev[0], axis=0, keepdims=True)
        out.append(m)
        hit = lev[0] == m
        for k in range(min(g - 1, count - r - 1)):
            lev[k] = jnp.where(hit, lev[k + 1], lev[k])
    if not with_rank:
        return out
    rank = jnp.zeros(x.shape, F32)
    for v in out:
        rank = rank + jnp.where(v > x, 1.0, 0.0)
    return out, rank


_PAIRS = [(a, b) for a in range(PEER_TOPK) for b in range(PEER_TOPK) if (a + 1) * (b + 1) <= PEER_TOPK]
_CAND_ROWS = -(-len(_PAIRS) // 8) * 8


_HI16 = 0xFFFF0000


def _bf16_hi_bits(x):
    u = lax.bitcast_convert_type(x, jnp.uint32)
    u = u + jnp.uint32(0x7FFF) + ((u >> 16) & jnp.uint32(1))
    return u & jnp.uint32(_HI16)


def _pack_pair(a, b):
    return _bf16_hi_bits(a) | (_bf16_hi_bits(b) >> 16)


def _sum_pair(w):
    return (lax.bitcast_convert_type(w & jnp.uint32(_HI16), F32)
            + lax.bitcast_convert_type(w << 16, F32))


def _head_tables(s1, s2, cand_s, tie_safe):
    k = PEER_TOPK
    lanes = s1.shape[-1]
    if tie_safe:
        v1 = _extract_top(s1, k)
        v2, rank2 = _extract_top(s2, k, with_rank=True)
    else:
        v1 = _top_distinct(s1, k)
        v2, rank2 = _top_distinct(s2, k, with_rank=True)
    cand_s[...] = jnp.full((_CAND_ROWS, lanes), -jnp.inf, F32)
    for r, (a, b) in enumerate(_PAIRS):
        cand_s[r:r + 1, :] = v1[a] + v2[b]
    cand = cand_s[...]
    top = _extract_top(cand, k, tie_safe=tie_safe)
    thr = top[k - 1]
    z = jnp.zeros_like(thr)
    for tk in top:
        z = z + jnp.exp(tk - top[0])
    count = lambda cond: jnp.sum(jnp.where(cond, 1.0, 0.0), axis=0, keepdims=True)
    if tie_safe:
        n1 = jnp.zeros_like(s1)
        for b in range(k):
            n1 = n1 + jnp.where(s1 + v2[b] >= thr, 1.0, 0.0)
        bad = None
    else:
        n1 = jnp.zeros_like(s1)
        for b in range(4):
            n1 = n1 + jnp.where(s1 + v2[b] >= thr, 1.0, 0.0)
        for a in range(3):
            n_a = jnp.zeros_like(thr)
            for b in range(k // (a + 1)):
                n_a = n_a + jnp.where(v1[a] + v2[b] >= thr, 1.0, 0.0)
            n1 = jnp.where(s1 == v1[a], n_a, n1)
        bad = ((count(s1 >= v1[k - 1]) != k) | (count(s2 >= v2[k - 1]) != k)
               | (count(cand >= thr) != k))
    return (rank2, jnp.exp(s2 - v2[0]), n1, jnp.exp(s1 - v1[0]) / z), bad


def _topk_kernel(st_ref, r2_ref, e2_ref, n1_ref, f1_ref, cand_s):
    refs = (r2_ref, e2_ref, n1_ref, f1_ref)
    for hp in range(PEER_HEADS // 2):
        def tables(tie_safe):
            ta, bad_a = _head_tables(st_ref[4 * hp], st_ref[4 * hp + 1], cand_s, tie_safe)
            tb, bad_b = _head_tables(st_ref[4 * hp + 2], st_ref[4 * hp + 3], cand_s, tie_safe)
            for ref, a, b in zip(refs, ta, tb):
                ref[hp] = _pack_pair(a, b)
            return None if tie_safe else jnp.max(jnp.where(bad_a | bad_b, 1.0, 0.0))

        any_bad = tables(tie_safe=False)

        @pl.when(any_bad > 0.0)
        def _():
            tables(tie_safe=True)


def _peer_topk(st, tt=256):
    t = st.shape[-1]
    tab = pl.BlockSpec((PEER_HEADS // 2, PEER_N_KEYS, tt), lambda i: (0, 0, i))
    out = jax.ShapeDtypeStruct((PEER_HEADS // 2, PEER_N_KEYS, t), jnp.uint32)
    return pl.pallas_call(
        _topk_kernel,
        grid=(t // tt,),
        in_specs=[pl.BlockSpec((2 * PEER_HEADS, PEER_N_KEYS, tt), lambda i: (0, 0, i))],
        out_specs=[tab] * 4,
        out_shape=[out] * 4,
        scratch_shapes=[pltpu.VMEM((_CAND_ROWS, tt), F32)],
        compiler_params=_params("parallel"),
        name="peer_topk",
    )(st)


_GELU_C0 = math.sqrt(2.0 / math.pi)
_GELU_C1 = 0.044715 * _GELU_C0


def _gelu_tanh(x):
    hx = 0.5 * x
    return hx + hx * jnp.tanh(x * (_GELU_C0 + _GELU_C1 * (x * x)))


def _peer_kernel(h2_ref, u_ref, vt_ref, r2_ref, e2_ref, n1_ref, f1_ref, x1_ref, x2_ref,
                 acc_s, ut_s, wt_s, *, ec, lw):
    j = pl.program_id(1)
    tt = h2_ref.shape[0]
    zero = jnp.zeros((), BF16)

    @pl.when(j == 0)
    def _():
        acc_s[...] = jnp.zeros_like(acc_s)

    ut_s[...] = _nt(pltpu.bitcast(u_ref[...], BF16), h2_ref[...])
    for r0 in range(0, ec, PEER_N_KEYS):
        q = r0 // PEER_N_KEYS
        grp = j * (ec // PEER_N_KEYS // 8) + q // 8
        sub = slice(q % 8, q % 8 + 1)
        for c0 in range(0, tt, lw):
            cs = slice(c0, c0 + lw)
            gate = None
            for hp in range(PEER_HEADS // 2):
                row = lambda ref: pltpu.bitcast(
                    jnp.broadcast_to(ref[hp, grp, sub, cs], (PEER_N_KEYS, lw)), BF16)
                r2 = pltpu.bitcast(r2_ref[hp, :, cs], BF16)
                e2 = pltpu.bitcast(e2_ref[hp, :, cs], BF16)
                term = jnp.where(r2 < row(n1_ref), e2 * row(f1_ref), zero)
                gate = term if gate is None else gate + term
            g = _sum_pair(pltpu.bitcast(gate, jnp.uint32))
            act = _gelu_tanh(ut_s[r0:r0 + PEER_N_KEYS, cs])
            wt_s[r0:r0 + PEER_N_KEYS, cs] = (g * act).astype(BF16)
    acc_s[...] += _mm(pltpu.bitcast(vt_ref[...], BF16), wt_s[...])

    @pl.when(j == pl.num_programs(1) - 1)
    def _():
        x2_ref[...] = x1_ref[...] + acc_s[...].T


def _peer_dense(h2, u, vt, r2, e2, n1, f1, x1):
    t = h2.shape[0]
    tt, ec = PEER_TT, PEER_EC
    n_exp = 2 * u.shape[0]
    tab = pl.BlockSpec((PEER_HEADS // 2, PEER_N_KEYS, tt), lambda i, j: (0, 0, i))
    tab1 = pl.BlockSpec((PEER_HEADS // 2, PEER_N_KEYS // 8, 8, tt), lambda i, j: (0, 0, 0, i))
    n1 = n1.reshape(PEER_HEADS // 2, PEER_N_KEYS // 8, 8, t)
    f1 = f1.reshape(PEER_HEADS // 2, PEER_N_KEYS // 8, 8, t)
    return pl.pallas_call(
        functools.partial(_peer_kernel, ec=ec, lw=PEER_LW),
        grid=(t // tt, n_exp // ec),
        in_specs=[
            pl.BlockSpec((tt, D_MODEL), lambda i, j: (i, 0)),
            pl.BlockSpec((ec // 2, D_MODEL), lambda i, j: (j, 0)),
            pl.BlockSpec((D_MODEL // 2, ec), lambda i, j: (0, j)),
            tab, tab, tab1, tab1,
            pl.BlockSpec((tt, D_MODEL), lambda i, j: (i, 0)),
        ],
        out_specs=pl.BlockSpec((tt, D_MODEL), lambda i, j: (i, 0)),
        out_shape=jax.ShapeDtypeStruct((t, D_MODEL), F32),
        scratch_shapes=[pltpu.VMEM((D_MODEL, tt), F32),
                        pltpu.VMEM((ec, tt), F32),
                        pltpu.VMEM((ec, tt), BF16)],
        compiler_params=_params("parallel", "arbitrary"),
        name="peer_dense",
    )(h2, u, vt, r2, e2, n1, f1, x1)


def _ple_kernel(x_ref, p_ref, g3_ref, wg_ref, wp_ref, gf_ref, y_ref, *, final_norm):
    x = x_ref[...]
    h3 = _rms(x, g3_ref[...]).astype(BF16)
    gate = jax.nn.sigmoid(_mm(h3, wg_ref[...]))
    x3 = x + _mm(p_ref[...].astype(BF16), wp_ref[...]) * gate
    y_ref[...] = _rms(x3, gf_ref[...]) if final_norm else x3


def _ple(x, p, g3, wg, wp, gf, final_norm, tm=512):
    t = x.shape[0]
    row = lambda c: pl.BlockSpec((tm, c), lambda i: (i, 0))
    return pl.pallas_call(
        functools.partial(_ple_kernel, final_norm=final_norm),
        grid=(t // tm,),
        in_specs=[row(D_MODEL), row(PLE_DIM)] + [_const_spec(a.shape) for a in (g3, wg, wp, gf)],
        out_specs=row(D_MODEL),
        out_shape=jax.ShapeDtypeStruct((t, D_MODEL), F32),
        compiler_params=_params("parallel"),
        name="ple",
    )(x, p, g3, wg, wp, gf)


def _t5_bucket(rel):
    nb = REL_BUCKETS // 2
    max_exact = nb // 2
    ret = (rel > 0).astype(jnp.int32) * nb
    n = jnp.abs(rel)
    nf = jnp.maximum(n, 1).astype(jnp.float32)
    large = max_exact + (jnp.log(nf / max_exact) / math.log(REL_MAX_DIST / max_exact)
                         * (nb - max_exact)).astype(jnp.int32)
    large = jnp.minimum(large, nb - 1)
    return ret + jnp.where(n < max_exact, n, large)


def _bias_table(rel_bias, tq):
    e = jnp.arange(5, dtype=jnp.int32)[:, None, None]
    r = jnp.arange(tq, dtype=jnp.int32)[None, :, None]
    c = jnp.arange(tq, dtype=jnp.int32)[None, None, :]
    bucket = _t5_bucket(r - c + (e - 2) * tq)[None]
    rb = rel_bias.astype(F32) * LOG2E
    tab = jnp.zeros((rb.shape[1], 5, tq, tq), F32)
    for k in range(REL_BUCKETS):
        tab = jnp.where(bucket == k, rb[k][:, None, None, None], tab)
    return tab


def _pack_row_pairs(w):
    bits = lax.bitcast_convert_type(w.astype(BF16), jnp.uint16).astype(jnp.uint32)
    bits = bits.reshape(w.shape[0] // 2, 2, w.shape[1])
    return bits[:, 0] | (bits[:, 1] << 16)


def _pack_in_proj(w_in, alpha_w, alpha_b):
    qa, ka, va, qg, kg, vg, rg, lr, gl = jnp.split(
        w_in, [512, 1024, 1536, 1792, 2048, 2560, 3072, 3104], axis=1)
    dup = lambda w: jnp.concatenate([w.reshape(-1, GLA_HEADS, 1, GLA_DK)] * 2, axis=2).reshape(-1, 512)
    lr_pad = jnp.pad(lr, ((0, 0), (0, LANES - lr.shape[1])))
    w = jnp.concatenate([qa * (DA_HEAD_DIM ** -0.5 * LOG2E), ka, dup(qg) * GLA_DK ** -0.5, dup(kg),
                         vg, rg, gl, lr_pad], axis=1).astype(BF16)
    aw = jnp.zeros((LANES, GLA_HEADS, 2, GLA_DK), F32)
    for d in range(2):
        aw = aw.at[d * GLA_GATE_RANK:(d + 1) * GLA_GATE_RANK, :, d, :].set(
            alpha_w[d].reshape(GLA_GATE_RANK, GLA_HEADS, GLA_DK))
    ab = alpha_b.reshape(2, GLA_HEADS, GLA_DK).transpose(1, 0, 2).reshape(1, 512)
    return w, va.T.astype(BF16), aw.reshape(LANES, 512).astype(BF16), ab.astype(F32)


def _trunk(x, p, rel_bias, norm1_g, w_in, lambda_qk, da_norm_g, gla_alpha_w, gla_alpha_b, gla_norm_g,
           w_up_a, w_up_b, w_out, norm2_g, peer_w_q, peer_sub_keys, peer_u, peer_v, norm3_g,
           ple_w, ple_gate_w, final_norm_g, prepared):
    batch, seq, _ = x.shape
    depth = w_in.shape[0]
    xf = x.reshape(batch * seq, D_MODEL)
    row = lambda v: v.reshape(1, -1).astype(F32)
    for i in range(depth):
        w1, wvt, aw, ab, bias_tab, u_bf, vt_bf, sk = prepared[i]
        lam_init = 0.8 - 0.6 * math.exp(-0.3 * i)
        qa, ka, vat, qg, kg, vg, rgs, gates, la = _in_proj(xf, row(norm1_g[i]), w1, wvt, aw, ab)
        oa = _diff_attn(qa, ka, vat, bias_tab, lambda_qk[i].astype(F32), row(da_norm_g[i]),
                        batch, seq, lam_init)
        og = _gla(qg, kg, vg, la, rgs, row(gla_norm_g[i]), batch, seq)
        x1, h2, st = _mix(oa, og, gates, xf, w_up_a[i].astype(BF16), w_up_b[i].astype(BF16),
                          w_out[i].astype(BF16), row(norm2_g[i]), peer_w_q[i].astype(BF16), sk)
        r2, e2, n1, f1 = _peer_topk(st)
        x2 = _peer_dense(h2, u_bf, vt_bf, r2, e2, n1, f1, x1)
        xf = _ple(x2, p[i].reshape(batch * seq, PLE_DIM), row(norm3_g[i]), ple_gate_w[i].astype(BF16),
                  ple_w[i].astype(BF16), row(final_norm_g), final_norm=(i == depth - 1))
    return xf.reshape(batch, seq, D_MODEL)


def kernel(x_prompt, x_sample, p_prompt, p_sample, rel_bias, norm1_g, w_in, lambda_qk, da_norm_g,
           gla_alpha_w, gla_alpha_b, gla_norm_g, w_up_a, w_up_b, w_out, norm2_g, peer_w_q,
           peer_sub_keys, peer_u, peer_v, norm3_g, ple_w, ple_gate_w, final_norm_g):
    prepared = []
    for i in range(w_in.shape[0]):
        w1, wvt, aw, ab = _pack_in_proj(w_in[i], gla_alpha_w[i], gla_alpha_b[i])
        prepared.append((w1, wvt, aw, ab, _bias_table(rel_bias, ATT_TQ), _pack_row_pairs(peer_u[i]),
                         _pack_row_pairs(peer_v[i].T),
                         peer_sub_keys[i].reshape(2 * PEER_HEADS, PEER_N_KEYS, -1).astype(BF16)))
    shared = (rel_bias, norm1_g, w_in, lambda_qk, da_norm_g, gla_alpha_w, gla_alpha_b, gla_norm_g,
              w_up_a, w_up_b, w_out, norm2_g, peer_w_q, peer_sub_keys, peer_u, peer_v, norm3_g,
              ple_w, ple_gate_w, final_norm_g)
    y_prompt = _trunk(x_prompt, p_prompt, *shared, prepared)
    y_sample = _trunk(x_sample, p_sample, *shared, prepared)
    return (y_prompt, y_sample)
```

```python
import functools
import math

import jax
import jax.numpy as jnp
from jax import lax
from jax.experimental import pallas as pl
from jax.experimental.pallas import tpu as pltpu

F32 = jnp.float32
BF16 = jnp.bfloat16

D_MODEL = 1024
DA_HEADS = 4
DA_HEAD_DIM = 64
DA_V_DIM = 128
GLA_HEADS = 4
GLA_DK = 64
GLA_DV = 128
GLA_GATE_RANK = 16
GLA_GATE_TAU = 16.0
GLA_CHUNK = 64
REL_BUCKETS = 32
REL_MAX_DIST = 128
PEER_HEADS = 8
PEER_N_KEYS = 128
PEER_TOPK = 16
PLE_DIM = 256
EPS = 1e-6

LANES = 128
V7X_VMEM_BYTES = 64 * 1024 * 1024
VMEM_LIMIT = V7X_VMEM_BYTES - 8 * 1024 * 1024

_C_QA, _C_KA, _C_QG, _C_KG, _C_VG, _C_RG, _C_GL, _C_LR, _C_END = (
    0, 512, 1024, 1536, 2048, 2560, 3072, 5120, 5248)

GLA_UNROLL = 8
GLA_GROUP = 4
ATT_TQ = 256
IN_PROJ_TM = 512
LOG2E = math.log2(math.e)
PEER_TT = 512
PEER_EC = 2048
PEER_LW = 512


def _nt(a, b):
    return lax.dot_general(a, b, (((1,), (1,)), ((), ())), preferred_element_type=F32)


def _tn(a, b):
    return lax.dot_general(a, b, (((0,), (0,)), ((), ())), preferred_element_type=F32)


def _mm(a, b):
    return jnp.dot(a, b, preferred_element_type=F32)


def _rms(x, g):
    return x * lax.rsqrt(jnp.mean(x * x, axis=-1, keepdims=True) + EPS) * g


def _params(*sem):
    return pltpu.CompilerParams(dimension_semantics=sem, vmem_limit_bytes=VMEM_LIMIT)


def _const_spec(shape):
    nd = len(shape)
    return pl.BlockSpec(shape, lambda *_: (0,) * nd)


def _in_proj_kernel(x_ref, g_ref, w_ref, wvt_ref, aw_ref, ab_ref,
                    qa_ref, ka_ref, vat_ref, qg_ref, kg_ref, vg_ref, rg_ref, gate_ref, la_ref):
    h = _rms(x_ref[...], g_ref[...]).astype(BF16)

    def proj(lo, hi):
        return _mm(h, w_ref[:, lo:hi])

    qa_ref[...] = proj(_C_QA, _C_KA).astype(BF16)
    ka_ref[...] = proj(_C_KA, _C_QG).astype(BF16)
    vat = _nt(wvt_ref[...], h).astype(BF16)
    for hd in range(DA_HEADS):
        for c in range(vat_ref.shape[1]):
            vat_ref[hd, c] = vat[hd * DA_V_DIM:(hd + 1) * DA_V_DIM, c * ATT_TQ:(c + 1) * ATT_TQ]
    qg_ref[...] = proj(_C_QG, _C_KG)
    kg_ref[...] = proj(_C_KG, _C_VG)
    vg_ref[...] = proj(_C_VG, _C_RG)
    rg = proj(_C_RG, _C_GL)
    rg_ref[...] = rg * jax.nn.sigmoid(rg)
    gate_ref[...] = jax.nn.sigmoid(proj(_C_GL, _C_LR)).astype(BF16)
    lr = proj(_C_LR, _C_END).astype(BF16)
    z = _mm(lr, aw_ref[...]) + ab_ref[...]
    la_ref[...] = (jnp.minimum(z, 0.0) - jnp.log1p(jnp.exp(-jnp.abs(z)))) * (1.0 / GLA_GATE_TAU)


def _in_proj(x, g, w, wvt, aw, ab):
    t = x.shape[0]
    tm = IN_PROJ_TM
    kc = tm // ATT_TQ
    row = lambda c: pl.BlockSpec((tm, c), lambda i: (i, 0))
    outs = [(512, BF16)] * 2 + [None] + [(512, F32)] * 4 + [(2048, BF16), (512, F32)]
    vat_spec = pl.BlockSpec((DA_HEADS, kc, DA_V_DIM, ATT_TQ), lambda i: (0, i, 0, 0))
    vat_shape = jax.ShapeDtypeStruct((DA_HEADS, t // ATT_TQ, DA_V_DIM, ATT_TQ), BF16)
    return pl.pallas_call(
        _in_proj_kernel,
        grid=(t // tm,),
        in_specs=[row(D_MODEL), _const_spec(g.shape), _const_spec(w.shape), _const_spec(wvt.shape),
                  _const_spec(aw.shape), _const_spec(ab.shape)],
        out_specs=[vat_spec if o is None else row(o[0]) for o in outs],
        out_shape=[vat_shape if o is None else jax.ShapeDtypeStruct((t, o[0]), o[1]) for o in outs],
        compiler_params=_params("parallel"),
        name="in_proj",
    )(x, g, w, wvt, aw, ab)


def _attn_kernel(q_ref, k_ref, vt_ref, bias_ref, lq_ref, g_ref, o_ref,
                 qb_s, s0_s, s1_s, p0_s, p1_s, a0_s, a1_s, m_s, l_s, acc_s, *, tq, nk, lam_init):
    i = pl.program_id(2)
    q = q_ref[...]
    lane = lax.broadcasted_iota(jnp.int32, q.shape, 1)
    zero = jnp.zeros_like(q)
    qb_s[0:tq, :] = jnp.where(lane < DA_HEAD_DIM, q, zero)
    qb_s[tq:2 * tq, :] = jnp.where(lane >= DA_HEAD_DIM, q, zero)
    m_s[...] = jnp.full(m_s.shape, -jnp.inf, F32)
    l_s[...] = jnp.zeros(l_s.shape, F32)
    acc_s[...] = jnp.zeros(acc_s.shape, F32)
    s_bufs, p_bufs, a_bufs = (s0_s, s1_s), (p0_s, p1_s), (a0_s, a1_s)

    def scores(t, buf):
        kc = k_ref[pl.ds(pl.multiple_of(t * tq, tq), tq), :]
        bias = bias_ref[jnp.clip(t - i, -2, 2) + 2]
        s = _nt(kc, qb_s[...])
        s_bufs[buf][:, 0:tq] = s[:, 0:tq] + bias
        s_bufs[buf][:, tq:2 * tq] = s[:, tq:2 * tq] + bias

    def softmax(buf):
        for c0 in range(0, 2 * tq, LANES):
            cs = slice(c0, c0 + LANES)
            s = s_bufs[buf][:, cs]
            m_old = m_s[:, cs]
            m_new = jnp.maximum(m_old, jnp.max(s, axis=0, keepdims=True))
            alpha = jnp.exp2(m_old - m_new)
            p = jnp.exp2(s - m_new)
            l_s[:, cs] = alpha * l_s[:, cs] + jnp.sum(p, axis=0, keepdims=True)
            m_s[:, cs] = m_new
            a_bufs[buf][:, cs] = alpha
            p_bufs[buf][:, cs] = p.astype(BF16)

    def values(t, buf):
        acc_s[...] = a_bufs[buf][...] * acc_s[...] + _mm(vt_ref[t], p_bufs[buf][...])

    scores(0, 0)
    scores(1, 1)
    softmax(0)

    def steady(u, carry):
        t = 2 + 2 * u
        scores(t, 0)
        softmax(1)
        values(t - 2, 0)
        scores(t + 1, 1)
        softmax(0)
        values(t - 1, 1)
        return carry

    lax.fori_loop(0, (nk - 2) // 2, steady, 0)
    softmax(1)
    values(nk - 2, 0)
    values(nk - 1, 1)

    lq = lq_ref[...]
    lam = (jnp.exp(jnp.sum(lq[0:1] * lq[1:2], axis=-1, keepdims=True))
           - jnp.exp(jnp.sum(lq[2:3] * lq[3:4], axis=-1, keepdims=True)) + lam_init)
    ot = acc_s[:, 0:tq] / l_s[:, 0:tq] - lam * (acc_s[:, tq:2 * tq] / l_s[:, tq:2 * tq])
    o_ref[...] = (_rms(ot.T, g_ref[...]) * (1.0 - lam_init)).astype(BF16)


def _diff_attn(qa, ka, vat, bias_tab, lq, g, batch, seq, lam_init):
    tq = ATT_TQ
    nq = seq // tq
    assert nq % 2 == 0 and nq >= 2
    stat = pltpu.VMEM((1, 2 * tq), F32)
    return pl.pallas_call(
        functools.partial(_attn_kernel, tq=tq, nk=nq, lam_init=lam_init),
        grid=(batch, DA_HEADS, nq),
        in_specs=[
            pl.BlockSpec((tq, LANES), lambda b, h, i: (b * nq + i, h)),
            pl.BlockSpec((seq, LANES), lambda b, h, i: (b, h)),
            pl.BlockSpec((None, nq, DA_V_DIM, tq), lambda b, h, i: (h, b, 0, 0)),
            pl.BlockSpec((None, 5, tq, tq), lambda b, h, i: (h, 0, 0, 0)),
            _const_spec(lq.shape),
            _const_spec(g.shape),
        ],
        out_specs=pl.BlockSpec((tq, LANES), lambda b, h, i: (b * nq + i, h)),
        out_shape=jax.ShapeDtypeStruct(qa.shape, BF16),
        scratch_shapes=[pltpu.VMEM((2 * tq, LANES), BF16),
                        pltpu.VMEM((tq, 2 * tq), F32), pltpu.VMEM((tq, 2 * tq), F32),
                        pltpu.VMEM((tq, 2 * tq), BF16), pltpu.VMEM((tq, 2 * tq), BF16),
                        stat, stat, stat, stat,
                        pltpu.VMEM((DA_V_DIM, 2 * tq), F32)],
        compiler_params=_params("parallel", "parallel", "arbitrary"),
        name="diff_attn",
    )(qa, ka, vat, bias_tab, lq, g)


def _split3(x):
    hi = x.astype(BF16)
    r = x - hi.astype(F32)
    mid = r.astype(BF16)
    lo = (r - mid.astype(F32)).astype(BF16)
    return hi, mid, lo


def _gla_kernel(q_ref, k_ref, v_ref, la_ref, rg_ref, g_ref, o_ref,
                qe_s, kv_s, dec_s, oacc_s, *, nc, rows):
    c = GLA_CHUNK
    gc = GLA_GROUP
    gr = gc * c
    shift = c.bit_length() - 1
    r_i = lax.broadcasted_iota(jnp.int32, (gr, gr), 0)
    c_i = lax.broadcasted_iota(jnp.int32, (gr, gr), 1)
    same_chunk = jnp.right_shift(r_i, shift) == jnp.right_shift(c_i, shift)
    pre = jnp.where(same_chunk & (c_i <= r_i), 1.0, 0.0).astype(BF16)
    suf = jnp.where(same_chunk & (c_i >= r_i), 1.0, 0.0).astype(BF16)
    causal = (lax.broadcasted_iota(jnp.int32, (c, c), 1)
              <= lax.broadcasted_iota(jnp.int32, (c, c), 0))
    fwd_lane_g = lax.broadcasted_iota(jnp.int32, (gr, LANES), 1) < GLA_DK
    fwd_lane = lax.broadcasted_iota(jnp.int32, (c, LANES), 1) < GLA_DK

    def group(gi, _):
        off = pl.multiple_of(gi * gr, gr)
        la = la_ref[pl.ds(off, gr), :]
        b = jnp.zeros((gr, LANES), F32)
        for part in _split3(jnp.where(fwd_lane_g, la, 0.0)):
            b = b + _mm(pre, part)
        for part in _split3(jnp.where(fwd_lane_g, 0.0, la)):
            b = b + _mm(suf, part)
        k = k_ref[pl.ds(off, gr), :]
        qe = q_ref[pl.ds(off, gr), :] * jnp.exp(b)
        qe_s[pl.ds(off, gr), :] = qe
        qeb = qe.astype(BF16)
        keb = (k * jnp.exp(-b)).astype(BF16)
        vb = v_ref[pl.ds(off, gr), :].astype(BF16)
        zero = jnp.zeros((c, LANES), BF16)
        for ci in range(gc):
            rs = slice(ci * c, (ci + 1) * c)
            b_last = jnp.where(fwd_lane[0:1], b[rs][c - 1:c], b[rs][0:1])
            kdb = (k[rs] * jnp.exp(b_last - b[rs])).astype(BF16)
            stacked = jnp.concatenate([jnp.where(fwd_lane, qeb[rs], zero),
                                       jnp.where(fwd_lane, zero, qeb[rs])], axis=0)
            a2 = _nt(stacked, keb[rs])
            att = jnp.where(causal, a2[0:c], a2[c:2 * c])
            oacc_s[pl.ds(off + ci * c, c), :] = _mm(att.astype(BF16), vb[rs])
            n = gi * gc + ci
            kv_s[n] = _tn(vb[rs], kdb)
            dec_s[n] = jnp.broadcast_to(jnp.exp(b_last), (8, LANES))
        return 0

    lax.fori_loop(0, nc // gc, group, 0, unroll=GLA_UNROLL // gc)

    def sweep(forward):
        def step(t, state):
            n = t if forward else nc - 1 - t
            off = pl.multiple_of(n * c, c)
            qe = qe_s[pl.ds(off, c), :]
            qm = jnp.where(fwd_lane, qe, 0.0) if forward else jnp.where(fwd_lane, 0.0, qe)
            oacc_s[pl.ds(off, c), :] += _nt(qm.astype(BF16), state.astype(BF16))
            return state * dec_s[n][0:1] + kv_s[n]
        lax.fori_loop(0, nc, step, jnp.zeros((GLA_DV, LANES), F32), unroll=GLA_UNROLL)

    sweep(True)
    sweep(False)

    def fin(r, _):
        off = pl.multiple_of(r * rows, rows)
        o = oacc_s[pl.ds(off, rows), :]
        o_ref[pl.ds(off, rows), :] = (_rms(o, g_ref[...]) * rg_ref[pl.ds(off, rows), :]).astype(BF16)
        return 0

    lax.fori_loop(0, (nc * c) // rows, fin, 0)


def _gla(qg, kg, vg, la, rgs, g, batch, seq):
    nc = seq // GLA_CHUNK
    blk = pl.BlockSpec((seq, LANES), lambda b, h: (b, h))
    return pl.pallas_call(
        functools.partial(_gla_kernel, nc=nc, rows=256),
        grid=(batch, GLA_HEADS),
        in_specs=[blk, blk, blk, blk, blk, _const_spec(g.shape)],
        out_specs=blk,
        out_shape=jax.ShapeDtypeStruct(vg.shape, BF16),
        scratch_shapes=[
            pltpu.VMEM((seq, LANES), F32),
            pltpu.VMEM((nc, GLA_DV, LANES), F32),
            pltpu.VMEM((nc, 8, LANES), F32),
            pltpu.VMEM((seq, LANES), F32),
        ],
        compiler_params=_params("parallel", "parallel"),
        name="gla",
    )(qg, kg, vg, la, rgs, g)


def _mix_kernel(oa_ref, og_ref, gate_ref, x_ref, wa_ref, wb_ref, wo_ref, g2_ref, wq_ref, sk_ref,
                x1_ref, h2_ref, st_ref):
    ya = _mm(oa_ref[...], wa_ref[...])
    yb = _mm(og_ref[...], wb_ref[...])
    merged = gate_ref[:, :D_MODEL].astype(F32) * ya + gate_ref[:, D_MODEL:].astype(F32) * yb
    x1 = x_ref[...] + _mm(merged.astype(BF16), wo_ref[...])
    x1_ref[...] = x1
    h2 = _rms(x1, g2_ref[...]).astype(BF16)
    h2_ref[...] = h2
    q = _mm(h2, wq_ref[...]).astype(BF16)
    for hc in range(2 * PEER_HEADS):
        st_ref[hc] = _nt(sk_ref[hc], q[:, hc * LANES:(hc + 1) * LANES])


def _mix(oa, og, gates, x, wa, wb, wo, g2, wq, sk, tm=512):
    t = x.shape[0]
    row = lambda c: pl.BlockSpec((tm, c), lambda i: (i, 0))
    return pl.pallas_call(
        _mix_kernel,
        grid=(t // tm,),
        in_specs=[row(512), row(512), row(2048), row(D_MODEL)]
        + [_const_spec(a.shape) for a in (wa, wb, wo, g2, wq, sk)],
        out_specs=[row(D_MODEL), row(D_MODEL),
                   pl.BlockSpec((2 * PEER_HEADS, PEER_N_KEYS, tm), lambda i: (0, 0, i))],
        out_shape=[jax.ShapeDtypeStruct((t, D_MODEL), F32),
                   jax.ShapeDtypeStruct((t, D_MODEL), BF16),
                   jax.ShapeDtypeStruct((2 * PEER_HEADS, PEER_N_KEYS, t), F32)],
        compiler_params=_params("parallel"),
        name="mix",
    )(oa, og, gates, x, wa, wb, wo, g2, wq, sk)


def _extract_top(cur, count, with_rank=False):
    rows = lax.broadcasted_iota(jnp.int32, cur.shape, 0)
    rank = jnp.full(cur.shape, float(count), F32)
    out = []
    for r in range(count):
        m = jnp.max(cur, axis=0, keepdims=True)
        out.append(m)
        hit = rows == jnp.min(jnp.where(cur == m, rows, cur.shape[0]), axis=0, keepdims=True)
        cur = jnp.where(hit, -jnp.inf, cur)
        if with_rank:
            rank = jnp.where(hit, float(r), rank)
    return (out, rank) if with_rank else out


def _oddeven_merge_sort_pairs(n):
    pairs = []
    p = 1
    while p < n:
        k = p
        while k >= 1:
            for j in range(k % p, n - k, 2 * k):
                for i in range(min(k, n - j - k)):
                    if (i + j) // (2 * p) == (i + j + k) // (2 * p):
                        pairs.append((i + j, i + j + k))
            k //= 2
        p *= 2
    return pairs


def _top_distinct(x, count, with_rank=False):
    g = x.shape[0] // 8
    lev = [x[8 * i:8 * (i + 1)] for i in range(g)]
    for i, j in _oddeven_merge_sort_pairs(g):
        lev[i], lev[j] = jnp.maximum(lev[i], lev[j]), jnp.minimum(lev[i], lev[j])
    lev.append(jnp.full_like(lev[0], -jnp.inf))
    out = []
    for r in range(count):
        m = jnp.max(lev[0], axis=0, keepdims=True)
        out.append(m)
        hit = lev[0] == m
        for k in range(min(g, count - r - 1)):
            lev[k] = jnp.where(hit, lev[k + 1], lev[k])
    if not with_rank:
        return out
    rank = jnp.zeros(x.shape, F32)
    for v in out:
        rank = rank + jnp.where(v > x, 1.0, 0.0)
    return out, rank


_PAIRS = [(a, b) for a in range(PEER_TOPK) for b in range(PEER_TOPK) if (a + 1) * (b + 1) <= PEER_TOPK]
_CAND_ROWS = 8 << (-(-len(_PAIRS) // 8) - 1).bit_length()


_HI16 = 0xFFFF0000


def _bf16_hi_bits(x):
    u = lax.bitcast_convert_type(x, jnp.uint32)
    u = u + jnp.uint32(0x7FFF) + ((u >> 16) & jnp.uint32(1))
    return u & jnp.uint32(_HI16)


def _pack_pair(a, b):
    return _bf16_hi_bits(a) | (_bf16_hi_bits(b) >> 16)


def _sum_pair(w):
    return (lax.bitcast_convert_type(w & jnp.uint32(_HI16), F32)
            + lax.bitcast_convert_type(w << 16, F32))


def _head_tables(s1, s2, cand_s, tie_safe):
    k = PEER_TOPK
    lanes = s1.shape[-1]
    if tie_safe:
        v1 = _extract_top(s1, k)
        v2, rank2 = _extract_top(s2, k, with_rank=True)
    else:
        v1 = _top_distinct(s1, k)
        v2, rank2 = _top_distinct(s2, k, with_rank=True)
    cand_s[...] = jnp.full((_CAND_ROWS, lanes), -jnp.inf, F32)
    for r, (a, b) in enumerate(_PAIRS):
        cand_s[r:r + 1, :] = v1[a] + v2[b]
    cand = cand_s[...]
    top = _extract_top(cand, k) if tie_safe else _top_distinct(cand, k)
    thr = top[k - 1]
    z = jnp.zeros_like(thr)
    for tk in top:
        z = z + jnp.exp(tk - top[0])
    count = lambda cond: jnp.sum(jnp.where(cond, 1.0, 0.0), axis=0, keepdims=True)
    if tie_safe:
        n1 = jnp.zeros_like(s1)
        for b in range(k):
            n1 = n1 + jnp.where(s1 + v2[b] >= thr, 1.0, 0.0)
        bad = None
    else:
        n1 = jnp.zeros_like(s1)
        for b in range(4):
            n1 = n1 + jnp.where(s1 + v2[b] >= thr, 1.0, 0.0)
        for a in range(3):
            n_a = jnp.zeros_like(thr)
            for b in range(k // (a + 1)):
                n_a = n_a + jnp.where(v1[a] + v2[b] >= thr, 1.0, 0.0)
            n1 = jnp.where(s1 == v1[a], n_a, n1)
        bad = ((count(s1 >= v1[k - 1]) != k) | (count(s2 >= v2[k - 1]) != k)
               | (count(cand >= thr) != k))
    return (rank2, jnp.exp(s2 - v2[0]), n1, jnp.exp(s1 - v1[0]) / z), bad


def _topk_kernel(st_ref, r2_ref, e2_ref, n1_ref, f1_ref, cand_s):
    refs = (r2_ref, e2_ref, n1_ref, f1_ref)
    for hp in range(PEER_HEADS // 2):
        def tables(tie_safe):
            ta, bad_a = _head_tables(st_ref[4 * hp], st_ref[4 * hp + 1], cand_s, tie_safe)
            tb, bad_b = _head_tables(st_ref[4 * hp + 2], st_ref[4 * hp + 3], cand_s, tie_safe)
            for ref, a, b in zip(refs, ta, tb):
                ref[hp] = _pack_pair(a, b)
            return None if tie_safe else jnp.max(jnp.where(bad_a | bad_b, 1.0, 0.0))

        any_bad = tables(tie_safe=False)

        @pl.when(any_bad > 0.0)
        def _():
            tables(tie_safe=True)


def _peer_topk(st, tt=256):
    t = st.shape[-1]
    tab = pl.BlockSpec((PEER_HEADS // 2, PEER_N_KEYS, tt), lambda i: (0, 0, i))
    out = jax.ShapeDtypeStruct((PEER_HEADS // 2, PEER_N_KEYS, t), jnp.uint32)
    return pl.pallas_call(
        _topk_kernel,
        grid=(t // tt,),
        in_specs=[pl.BlockSpec((2 * PEER_HEADS, PEER_N_KEYS, tt), lambda i: (0, 0, i))],
        out_specs=[tab] * 4,
        out_shape=[out] * 4,
        scratch_shapes=[pltpu.VMEM((_CAND_ROWS, tt), F32)],
        compiler_params=_params("parallel"),
        name="peer_topk",
    )(st)


_GELU_C0 = math.sqrt(2.0 / math.pi)
_GELU_C1 = 0.044715 * _GELU_C0


def _gelu_tanh(x):
    hx = 0.5 * x
    return hx + hx * jnp.tanh(x * (_GELU_C0 + _GELU_C1 * (x * x)))


def _peer_kernel(h2_ref, u_ref, vt_ref, r2_ref, e2_ref, n1_ref, f1_ref, x1_ref, x2_ref,
                 acc_s, ut_s, wt_s, *, ec, lw):
    j = pl.program_id(1)
    tt = h2_ref.shape[0]
    zero = jnp.zeros((), BF16)

    @pl.when(j == 0)
    def _():
        acc_s[...] = jnp.zeros_like(acc_s)

    ut_s[...] = _nt(pltpu.bitcast(u_ref[...], BF16), h2_ref[...])
    for r0 in range(0, ec, PEER_N_KEYS):
        q = r0 // PEER_N_KEYS
        grp = j * (ec // PEER_N_KEYS // 8) + q // 8
        sub = slice(q % 8, q % 8 + 1)
        for c0 in range(0, tt, lw):
            cs = slice(c0, c0 + lw)
            gate = None
            for hp in range(PEER_HEADS // 2):
                row = lambda ref: pltpu.bitcast(
                    jnp.broadcast_to(ref[hp, grp, sub, cs], (PEER_N_KEYS, lw)), BF16)
                r2 = pltpu.bitcast(r2_ref[hp, :, cs], BF16)
                e2 = pltpu.bitcast(e2_ref[hp, :, cs], BF16)
                term = jnp.where(r2 < row(n1_ref), e2 * row(f1_ref), zero)
                gate = term if gate is None else gate + term
            g = _sum_pair(pltpu.bitcast(gate, jnp.uint32))
            act = _gelu_tanh(ut_s[r0:r0 + PEER_N_KEYS, cs])
            wt_s[r0:r0 + PEER_N_KEYS, cs] = (g * act).astype(BF16)
    acc_s[...] += _mm(pltpu.bitcast(vt_ref[...], BF16), wt_s[...])

    @pl.when(j == pl.num_programs(1) - 1)
    def _():
        x2_ref[...] = x1_ref[...] + acc_s[...].T


def _peer_dense(h2, u, vt, r2, e2, n1, f1, x1):
    t = h2.shape[0]
    tt, ec = PEER_TT, PEER_EC
    n_exp = 2 * u.shape[0]
    tab = pl.BlockSpec((PEER_HEADS // 2, PEER_N_KEYS, tt), lambda i, j: (0, 0, i))
    tab1 = pl.BlockSpec((PEER_HEADS // 2, PEER_N_KEYS // 8, 8, tt), lambda i, j: (0, 0, 0, i))
    n1 = n1.reshape(PEER_HEADS // 2, PEER_N_KEYS // 8, 8, t)
    f1 = f1.reshape(PEER_HEADS // 2, PEER_N_KEYS // 8, 8, t)
    return pl.pallas_call(
        functools.partial(_peer_kernel, ec=ec, lw=PEER_LW),
        grid=(t // tt, n_exp // ec),
        in_specs=[
            pl.BlockSpec((tt, D_MODEL), lambda i, j: (i, 0)),
            pl.BlockSpec((ec // 2, D_MODEL), lambda i, j: (j, 0)),
            pl.BlockSpec((D_MODEL // 2, ec), lambda i, j: (0, j)),
            tab, tab, tab1, tab1,
            pl.BlockSpec((tt, D_MODEL), lambda i, j: (i, 0)),
        ],
        out_specs=pl.BlockSpec((tt, D_MODEL), lambda i, j: (i, 0)),
        out_shape=jax.ShapeDtypeStruct((t, D_MODEL), F32),
        scratch_shapes=[pltpu.VMEM((D_MODEL, tt), F32),
                        pltpu.VMEM((ec, tt), F32),
                        pltpu.VMEM((ec, tt), BF16)],
        compiler_params=_params("parallel", "arbitrary"),
        name="peer_dense",
    )(h2, u, vt, r2, e2, n1, f1, x1)


def _ple_kernel(x_ref, p_ref, g3_ref, wg_ref, wp_ref, gf_ref, y_ref, *, final_norm):
    x = x_ref[...]
    h3 = _rms(x, g3_ref[...]).astype(BF16)
    gate = jax.nn.sigmoid(_mm(h3, wg_ref[...]))
    x3 = x + _mm(p_ref[...].astype(BF16), wp_ref[...]) * gate
    y_ref[...] = _rms(x3, gf_ref[...]) if final_norm else x3


def _ple(x, p, g3, wg, wp, gf, final_norm, tm=512):
    t = x.shape[0]
    row = lambda c: pl.BlockSpec((tm, c), lambda i: (i, 0))
    return pl.pallas_call(
        functools.partial(_ple_kernel, final_norm=final_norm),
        grid=(t // tm,),
        in_specs=[row(D_MODEL), row(PLE_DIM)] + [_const_spec(a.shape) for a in (g3, wg, wp, gf)],
        out_specs=row(D_MODEL),
        out_shape=jax.ShapeDtypeStruct((t, D_MODEL), F32),
        compiler_params=_params("parallel"),
        name="ple",
    )(x, p, g3, wg, wp, gf)


def _t5_bucket(rel):
    nb = REL_BUCKETS // 2
    max_exact = nb // 2
    ret = (rel > 0).astype(jnp.int32) * nb
    n = jnp.abs(rel)
    nf = jnp.maximum(n, 1).astype(jnp.float32)
    large = max_exact + (jnp.log(nf / max_exact) / math.log(REL_MAX_DIST / max_exact)
                         * (nb - max_exact)).astype(jnp.int32)
    large = jnp.minimum(large, nb - 1)
    return ret + jnp.where(n < max_exact, n, large)


def _bias_table(rel_bias, tq):
    e = jnp.arange(5, dtype=jnp.int32)[:, None, None]
    r = jnp.arange(tq, dtype=jnp.int32)[None, :, None]
    c = jnp.arange(tq, dtype=jnp.int32)[None, None, :]
    bucket = _t5_bucket(r - c + (e - 2) * tq)[None]
    rb = rel_bias.astype(F32) * LOG2E
    tab = jnp.zeros((rb.shape[1], 5, tq, tq), F32)
    for k in range(REL_BUCKETS):
        tab = jnp.where(bucket == k, rb[k][:, None, None, None], tab)
    return tab


def _pack_row_pairs(w):
    bits = lax.bitcast_convert_type(w.astype(BF16), jnp.uint16).astype(jnp.uint32)
    bits = bits.reshape(w.shape[0] // 2, 2, w.shape[1])
    return bits[:, 0] | (bits[:, 1] << 16)


def _pack_in_proj(w_in, alpha_w, alpha_b):
    qa, ka, va, qg, kg, vg, rg, lr, gl = jnp.split(
        w_in, [512, 1024, 1536, 1792, 2048, 2560, 3072, 3104], axis=1)
    dup = lambda w: jnp.concatenate([w.reshape(-1, GLA_HEADS, 1, GLA_DK)] * 2, axis=2).reshape(-1, 512)
    lr_pad = jnp.pad(lr, ((0, 0), (0, LANES - lr.shape[1])))
    w = jnp.concatenate([qa * (DA_HEAD_DIM ** -0.5 * LOG2E), ka, dup(qg) * GLA_DK ** -0.5, dup(kg),
                         vg, rg, gl, lr_pad], axis=1).astype(BF16)
    aw = jnp.zeros((LANES, GLA_HEADS, 2, GLA_DK), F32)
    for d in range(2):
        aw = aw.at[d * GLA_GATE_RANK:(d + 1) * GLA_GATE_RANK, :, d, :].set(
            alpha_w[d].reshape(GLA_GATE_RANK, GLA_HEADS, GLA_DK))
    ab = alpha_b.reshape(2, GLA_HEADS, GLA_DK).transpose(1, 0, 2).reshape(1, 512)
    return w, va.T.astype(BF16), aw.reshape(LANES, 512).astype(BF16), ab.astype(F32)


def _trunk(x, p, rel_bias, norm1_g, w_in, lambda_qk, da_norm_g, gla_alpha_w, gla_alpha_b, gla_norm_g,
           w_up_a, w_up_b, w_out, norm2_g, peer_w_q, peer_sub_keys, peer_u, peer_v, norm3_g,
           ple_w, ple_gate_w, final_norm_g, prepared):
    batch, seq, _ = x.shape
    depth = w_in.shape[0]
    xf = x.reshape(batch * seq, D_MODEL)
    row = lambda v: v.reshape(1, -1).astype(F32)
    for i in range(depth):
        w1, wvt, aw, ab, bias_tab, u_bf, vt_bf, sk = prepared[i]
        lam_init = 0.8 - 0.6 * math.exp(-0.3 * i)
        qa, ka, vat, qg, kg, vg, rgs, gates, la = _in_proj(xf, row(norm1_g[i]), w1, wvt, aw, ab)
        oa = _diff_attn(qa, ka, vat, bias_tab, lambda_qk[i].astype(F32), row(da_norm_g[i]),
                        batch, seq, lam_init)
        og = _gla(qg, kg, vg, la, rgs, row(gla_norm_g[i]), batch, seq)
        x1, h2, st = _mix(oa, og, gates, xf, w_up_a[i].astype(BF16), w_up_b[i].astype(BF16),
                          w_out[i].astype(BF16), row(norm2_g[i]), peer_w_q[i].astype(BF16), sk)
        r2, e2, n1, f1 = _peer_topk(st)
        x2 = _peer_dense(h2, u_bf, vt_bf, r2, e2, n1, f1, x1)
        xf = _ple(x2, p[i].reshape(batch * seq, PLE_DIM), row(norm3_g[i]), ple_gate_w[i].astype(BF16),
                  ple_w[i].astype(BF16), row(final_norm_g), final_norm=(i == depth - 1))
    return xf.reshape(batch, seq, D_MODEL)


def kernel(x_prompt, x_sample, p_prompt, p_sample, rel_bias, norm1_g, w_in, lambda_qk, da_norm_g,
           gla_alpha_w, gla_alpha_b, gla_norm_g, w_up_a, w_up_b, w_out, norm2_g, peer_w_q,
           peer_sub_keys, peer_u, peer_v, norm3_g, ple_w, ple_gate_w, final_norm_g):
    prepared = []
    for i in range(w_in.shape[0]):
        w1, wvt, aw, ab = _pack_in_proj(w_in[i], gla_alpha_w[i], gla_alpha_b[i])
        prepared.append((w1, wvt, aw, ab, _bias_table(rel_bias, ATT_TQ), _pack_row_pairs(peer_u[i]),
                         _pack_row_pairs(peer_v[i].T),
                         peer_sub_keys[i].reshape(2 * PEER_HEADS, PEER_N_KEYS, -1).astype(BF16)))
    shared = (rel_bias, norm1_g, w_in, lambda_qk, da_norm_g, gla_alpha_w, gla_alpha_b, gla_norm_g,
              w_up_a, w_up_b, w_out, norm2_g, peer_w_q, peer_sub_keys, peer_u, peer_v, norm3_g,
              ple_w, ple_gate_w, final_norm_g)
    y_prompt = _trunk(x_prompt, p_prompt, *shared, prepared)
    y_sample = _trunk(x_sample, p_sample, *shared, prepared)
    return (y_prompt, y_sample)
```

```python
import functools
import math

import jax
import jax.numpy as jnp
from jax import lax
from jax.experimental import pallas as pl
from jax.experimental.pallas import tpu as pltpu

F32 = jnp.float32
BF16 = jnp.bfloat16

D_MODEL = 1024
DA_HEADS = 4
DA_HEAD_DIM = 64
DA_V_DIM = 128
GLA_HEADS = 4
GLA_DK = 64
GLA_DV = 128
GLA_GATE_RANK = 16
GLA_GATE_TAU = 16.0
GLA_CHUNK = 64
REL_BUCKETS = 32
REL_MAX_DIST = 128
PEER_HEADS = 8
PEER_N_KEYS = 128
PEER_TOPK = 16
PLE_DIM = 256
EPS = 1e-6

LANES = 128
V7X_VMEM_BYTES = 64 * 1024 * 1024
VMEM_LIMIT = V7X_VMEM_BYTES - 8 * 1024 * 1024

_C_QA, _C_KA, _C_QG, _C_KG, _C_VG, _C_RG, _C_GL, _C_LR, _C_END = (
    0, 512, 1024, 1536, 2048, 2560, 3072, 5120, 5248)

GLA_UNROLL = 8
GLA_GROUP = 4
ATT_TQ = 256
IN_PROJ_TM = 512
LOG2E = math.log2(math.e)
PEER_TT = 512
PEER_EC = 2048
PEER_LW = 128


def _nt(a, b):
    return lax.dot_general(a, b, (((1,), (1,)), ((), ())), preferred_element_type=F32)


def _tn(a, b):
    return lax.dot_general(a, b, (((0,), (0,)), ((), ())), preferred_element_type=F32)


def _mm(a, b):
    return jnp.dot(a, b, preferred_element_type=F32)


def _rms(x, g):
    return x * lax.rsqrt(jnp.mean(x * x, axis=-1, keepdims=True) + EPS) * g


def _params(*sem):
    return pltpu.CompilerParams(dimension_semantics=sem, vmem_limit_bytes=VMEM_LIMIT)


def _const_spec(shape):
    nd = len(shape)
    return pl.BlockSpec(shape, lambda *_: (0,) * nd)


def _in_proj_kernel(x_ref, g_ref, w_ref, wvt_ref, aw_ref, ab_ref,
                    qa_ref, ka_ref, vat_ref, qg_ref, kg_ref, vg_ref, rg_ref, gate_ref, la_ref):
    h = _rms(x_ref[...], g_ref[...]).astype(BF16)

    def proj(lo, hi):
        return _mm(h, w_ref[:, lo:hi])

    qa_ref[...] = proj(_C_QA, _C_KA).astype(BF16)
    ka_ref[...] = proj(_C_KA, _C_QG).astype(BF16)
    vat = _nt(wvt_ref[...], h).astype(BF16)
    for hd in range(DA_HEADS):
        for c in range(vat_ref.shape[1]):
            vat_ref[hd, c] = vat[hd * DA_V_DIM:(hd + 1) * DA_V_DIM, c * ATT_TQ:(c + 1) * ATT_TQ]
    qg_ref[...] = proj(_C_QG, _C_KG)
    kg_ref[...] = proj(_C_KG, _C_VG)
    vg_ref[...] = proj(_C_VG, _C_RG)
    rg = proj(_C_RG, _C_GL)
    rg_ref[...] = rg * jax.nn.sigmoid(rg)
    gate_ref[...] = jax.nn.sigmoid(proj(_C_GL, _C_LR)).astype(BF16)
    lr = proj(_C_LR, _C_END).astype(BF16)
    z = _mm(lr, aw_ref[...]) + ab_ref[...]
    la_ref[...] = (jnp.minimum(z, 0.0) - jnp.log1p(jnp.exp(-jnp.abs(z)))) * (1.0 / GLA_GATE_TAU)


def _in_proj(x, g, w, wvt, aw, ab):
    t = x.shape[0]
    tm = IN_PROJ_TM
    kc = tm // ATT_TQ
    row = lambda c: pl.BlockSpec((tm, c), lambda i: (i, 0))
    outs = [(512, BF16)] * 2 + [None] + [(512, F32)] * 4 + [(2048, BF16), (512, F32)]
    vat_spec = pl.BlockSpec((DA_HEADS, kc, DA_V_DIM, ATT_TQ), lambda i: (0, i, 0, 0))
    vat_shape = jax.ShapeDtypeStruct((DA_HEADS, t // ATT_TQ, DA_V_DIM, ATT_TQ), BF16)
    return pl.pallas_call(
        _in_proj_kernel,
        grid=(t // tm,),
        in_specs=[row(D_MODEL), _const_spec(g.shape), _const_spec(w.shape), _const_spec(wvt.shape),
                  _const_spec(aw.shape), _const_spec(ab.shape)],
        out_specs=[vat_spec if o is None else row(o[0]) for o in outs],
        out_shape=[vat_shape if o is None else jax.ShapeDtypeStruct((t, o[0]), o[1]) for o in outs],
        compiler_params=_params("parallel"),
        name="in_proj",
    )(x, g, w, wvt, aw, ab)


def _attn_kernel(q_ref, k_ref, vt_ref, bias_ref, lq_ref, g_ref, o_ref,
                 qb_s, s0_s, s1_s, p0_s, p1_s, a0_s, a1_s, m_s, l_s, acc_s, *, tq, nk, lam_init):
    i = pl.program_id(2)
    q = q_ref[...]
    lane = lax.broadcasted_iota(jnp.int32, q.shape, 1)
    zero = jnp.zeros_like(q)
    qb_s[0:tq, :] = jnp.where(lane < DA_HEAD_DIM, q, zero)
    qb_s[tq:2 * tq, :] = jnp.where(lane >= DA_HEAD_DIM, q, zero)
    m_s[...] = jnp.full(m_s.shape, -jnp.inf, F32)
    l_s[...] = jnp.zeros(l_s.shape, F32)
    acc_s[...] = jnp.zeros(acc_s.shape, F32)
    s_bufs, p_bufs, a_bufs = (s0_s, s1_s), (p0_s, p1_s), (a0_s, a1_s)

    def scores(t, buf):
        kc = k_ref[pl.ds(pl.multiple_of(t * tq, tq), tq), :]
        bias = bias_ref[jnp.clip(t - i, -2, 2) + 2]
        s = _nt(kc, qb_s[...])
        s_bufs[buf][:, 0:tq] = s[:, 0:tq] + bias
        s_bufs[buf][:, tq:2 * tq] = s[:, tq:2 * tq] + bias

    def softmax(buf):
        for c0 in range(0, 2 * tq, LANES):
            cs = slice(c0, c0 + LANES)
            s = s_bufs[buf][:, cs]
            m_old = m_s[:, cs]
            m_new = jnp.maximum(m_old, jnp.max(s, axis=0, keepdims=True))
            alpha = jnp.exp2(m_old - m_new)
            p = jnp.exp2(s - m_new)
            l_s[:, cs] = alpha * l_s[:, cs] + jnp.sum(p, axis=0, keepdims=True)
            m_s[:, cs] = m_new
            a_bufs[buf][:, cs] = alpha
            p_bufs[buf][:, cs] = p.astype(BF16)

    def values(t, buf):
        acc_s[...] = a_bufs[buf][...] * acc_s[...] + _mm(vt_ref[t], p_bufs[buf][...])

    scores(0, 0)
    scores(1, 1)
    softmax(0)

    def steady(u, carry):
        t = 2 + 2 * u
        scores(t, 0)
        softmax(1)
        values(t - 2, 0)
        scores(t + 1, 1)
        softmax(0)
        values(t - 1, 1)
        return carry

    lax.fori_loop(0, (nk - 2) // 2, steady, 0)
    softmax(1)
    values(nk - 2, 0)
    values(nk - 1, 1)

    lq = lq_ref[...]
    lam = (jnp.exp(jnp.sum(lq[0:1] * lq[1:2], axis=-1, keepdims=True))
           - jnp.exp(jnp.sum(lq[2:3] * lq[3:4], axis=-1, keepdims=True)) + lam_init)
    ot = acc_s[:, 0:tq] / l_s[:, 0:tq] - lam * (acc_s[:, tq:2 * tq] / l_s[:, tq:2 * tq])
    o_ref[...] = (_rms(ot.T, g_ref[...]) * (1.0 - lam_init)).astype(BF16)


def _diff_attn(qa, ka, vat, bias_tab, lq, g, batch, seq, lam_init):
    tq = ATT_TQ
    nq = seq // tq
    assert nq % 2 == 0 and nq >= 2
    stat = pltpu.VMEM((1, 2 * tq), F32)
    return pl.pallas_call(
        functools.partial(_attn_kernel, tq=tq, nk=nq, lam_init=lam_init),
        grid=(batch, DA_HEADS, nq),
        in_specs=[
            pl.BlockSpec((tq, LANES), lambda b, h, i: (b * nq + i, h)),
            pl.BlockSpec((seq, LANES), lambda b, h, i: (b, h)),
            pl.BlockSpec((None, nq, DA_V_DIM, tq), lambda b, h, i: (h, b, 0, 0)),
            pl.BlockSpec((None, 5, tq, tq), lambda b, h, i: (h, 0, 0, 0)),
            _const_spec(lq.shape),
            _const_spec(g.shape),
        ],
        out_specs=pl.BlockSpec((tq, LANES), lambda b, h, i: (b * nq + i, h)),
        out_shape=jax.ShapeDtypeStruct(qa.shape, BF16),
        scratch_shapes=[pltpu.VMEM((2 * tq, LANES), BF16),
                        pltpu.VMEM((tq, 2 * tq), F32), pltpu.VMEM((tq, 2 * tq), F32),
                        pltpu.VMEM((tq, 2 * tq), BF16), pltpu.VMEM((tq, 2 * tq), BF16),
                        stat, stat, stat, stat,
                        pltpu.VMEM((DA_V_DIM, 2 * tq), F32)],
        compiler_params=_params("parallel", "parallel", "arbitrary"),
        name="diff_attn",
    )(qa, ka, vat, bias_tab, lq, g)


def _split3(x):
    hi = x.astype(BF16)
    r = x - hi.astype(F32)
    mid = r.astype(BF16)
    lo = (r - mid.astype(F32)).astype(BF16)
    return hi, mid, lo


def _gla_kernel(q_ref, k_ref, v_ref, la_ref, rg_ref, g_ref, o_ref,
                qe_s, kv_s, dec_s, oacc_s, *, nc, rows):
    c = GLA_CHUNK
    gc = GLA_GROUP
    gr = gc * c
    shift = c.bit_length() - 1
    r_i = lax.broadcasted_iota(jnp.int32, (gr, gr), 0)
    c_i = lax.broadcasted_iota(jnp.int32, (gr, gr), 1)
    same_chunk = jnp.right_shift(r_i, shift) == jnp.right_shift(c_i, shift)
    pre = jnp.where(same_chunk & (c_i <= r_i), 1.0, 0.0).astype(BF16)
    suf = jnp.where(same_chunk & (c_i >= r_i), 1.0, 0.0).astype(BF16)
    causal = (lax.broadcasted_iota(jnp.int32, (c, c), 1)
              <= lax.broadcasted_iota(jnp.int32, (c, c), 0))
    fwd_lane_g = lax.broadcasted_iota(jnp.int32, (gr, LANES), 1) < GLA_DK
    fwd_lane = lax.broadcasted_iota(jnp.int32, (c, LANES), 1) < GLA_DK

    def group(gi, _):
        off = pl.multiple_of(gi * gr, gr)
        la = la_ref[pl.ds(off, gr), :]
        b = jnp.zeros((gr, LANES), F32)
        for part in _split3(jnp.where(fwd_lane_g, la, 0.0)):
            b = b + _mm(pre, part)
        for part in _split3(jnp.where(fwd_lane_g, 0.0, la)):
            b = b + _mm(suf, part)
        k = k_ref[pl.ds(off, gr), :]
        qe = q_ref[pl.ds(off, gr), :] * jnp.exp(b)
        qe_s[pl.ds(off, gr), :] = qe
        qeb = qe.astype(BF16)
        keb = (k * jnp.exp(-b)).astype(BF16)
        vb = v_ref[pl.ds(off, gr), :].astype(BF16)
        zero = jnp.zeros((c, LANES), BF16)
        for ci in range(gc):
            rs = slice(ci * c, (ci + 1) * c)
            b_last = jnp.where(fwd_lane[0:1], b[rs][c - 1:c], b[rs][0:1])
            kdb = (k[rs] * jnp.exp(b_last - b[rs])).astype(BF16)
            stacked = jnp.concatenate([jnp.where(fwd_lane, qeb[rs], zero),
                                       jnp.where(fwd_lane, zero, qeb[rs])], axis=0)
            a2 = _nt(stacked, keb[rs])
            att = jnp.where(causal, a2[0:c], a2[c:2 * c])
            oacc_s[pl.ds(off + ci * c, c), :] = _mm(att.astype(BF16), vb[rs])
            n = gi * gc + ci
            kv_s[n] = _tn(vb[rs], kdb)
            dec_s[n] = jnp.broadcast_to(jnp.exp(b_last), (8, LANES))
        return 0

    lax.fori_loop(0, nc // gc, group, 0, unroll=GLA_UNROLL // gc)

    def sweep(forward):
        def step(t, state):
            n = t if forward else nc - 1 - t
            off = pl.multiple_of(n * c, c)
            qe = qe_s[pl.ds(off, c), :]
            qm = jnp.where(fwd_lane, qe, 0.0) if forward else jnp.where(fwd_lane, 0.0, qe)
            oacc_s[pl.ds(off, c), :] += _nt(qm.astype(BF16), state.astype(BF16))
            return state * dec_s[n][0:1] + kv_s[n]
        lax.fori_loop(0, nc, step, jnp.zeros((GLA_DV, LANES), F32), unroll=GLA_UNROLL)

    sweep(True)
    sweep(False)

    def fin(r, _):
        off = pl.multiple_of(r * rows, rows)
        o = oacc_s[pl.ds(off, rows), :]
        o_ref[pl.ds(off, rows), :] = (_rms(o, g_ref[...]) * rg_ref[pl.ds(off, rows), :]).astype(BF16)
        return 0

    lax.fori_loop(0, (nc * c) // rows, fin, 0)


def _gla(qg, kg, vg, la, rgs, g, batch, seq):
    nc = seq // GLA_CHUNK
    blk = pl.BlockSpec((seq, LANES), lambda b, h: (b, h))
    return pl.pallas_call(
        functools.partial(_gla_kernel, nc=nc, rows=256),
        grid=(batch, GLA_HEADS),
        in_specs=[blk, blk, blk, blk, blk, _const_spec(g.shape)],
        out_specs=blk,
        out_shape=jax.ShapeDtypeStruct(vg.shape, BF16),
        scratch_shapes=[
            pltpu.VMEM((seq, LANES), F32),
            pltpu.VMEM((nc, GLA_DV, LANES), F32),
            pltpu.VMEM((nc, 8, LANES), F32),
            pltpu.VMEM((seq, LANES), F32),
        ],
        compiler_params=_params("parallel", "parallel"),
        name="gla",
    )(qg, kg, vg, la, rgs, g)


def _mix_kernel(oa_ref, og_ref, gate_ref, x_ref, wa_ref, wb_ref, wo_ref, g2_ref, wq_ref, sk_ref,
                x1_ref, h2_ref, st_ref):
    ya = _mm(oa_ref[...], wa_ref[...])
    yb = _mm(og_ref[...], wb_ref[...])
    merged = gate_ref[:, :D_MODEL].astype(F32) * ya + gate_ref[:, D_MODEL:].astype(F32) * yb
    x1 = x_ref[...] + _mm(merged.astype(BF16), wo_ref[...])
    x1_ref[...] = x1
    h2 = _rms(x1, g2_ref[...]).astype(BF16)
    h2_ref[...] = h2
    q = _mm(h2, wq_ref[...]).astype(BF16)
    for hc in range(2 * PEER_HEADS):
        st_ref[hc] = _nt(sk_ref[hc], q[:, hc * LANES:(hc + 1) * LANES])


def _mix(oa, og, gates, x, wa, wb, wo, g2, wq, sk, tm=512):
    t = x.shape[0]
    row = lambda c: pl.BlockSpec((tm, c), lambda i: (i, 0))
    return pl.pallas_call(
        _mix_kernel,
        grid=(t // tm,),
        in_specs=[row(512), row(512), row(2048), row(D_MODEL)]
        + [_const_spec(a.shape) for a in (wa, wb, wo, g2, wq, sk)],
        out_specs=[row(D_MODEL), row(D_MODEL),
                   pl.BlockSpec((2 * PEER_HEADS, PEER_N_KEYS, tm), lambda i: (0, 0, i))],
        out_shape=[jax.ShapeDtypeStruct((t, D_MODEL), F32),
                   jax.ShapeDtypeStruct((t, D_MODEL), BF16),
                   jax.ShapeDtypeStruct((2 * PEER_HEADS, PEER_N_KEYS, t), F32)],
        compiler_params=_params("parallel"),
        name="mix",
    )(oa, og, gates, x, wa, wb, wo, g2, wq, sk)


def _extract_top(cur, count, with_rank=False):
    rows = lax.broadcasted_iota(jnp.int32, cur.shape, 0)
    rank = jnp.full(cur.shape, float(count), F32)
    out = []
    for r in range(count):
        m = jnp.max(cur, axis=0, keepdims=True)
        out.append(m)
        hit = rows == jnp.min(jnp.where(cur == m, rows, cur.shape[0]), axis=0, keepdims=True)
        cur = jnp.where(hit, -jnp.inf, cur)
        if with_rank:
            rank = jnp.where(hit, float(r), rank)
    return (out, rank) if with_rank else out


def _oddeven_merge_sort_pairs(n):
    pairs = []
    p = 1
    while p < n:
        k = p
        while k >= 1:
            for j in range(k % p, n - k, 2 * k):
                for i in range(min(k, n - j - k)):
                    if (i + j) // (2 * p) == (i + j + k) // (2 * p):
                        pairs.append((i + j, i + j + k))
            k //= 2
        p *= 2
    return pairs


def _top_distinct(x, count, with_rank=False):
    g = x.shape[0] // 8
    lev = [x[8 * i:8 * (i + 1)] for i in range(g)]
    for i, j in _oddeven_merge_sort_pairs(g):
        lev[i], lev[j] = jnp.maximum(lev[i], lev[j]), jnp.minimum(lev[i], lev[j])
    lev.append(jnp.full_like(lev[0], -jnp.inf))
    out = []
    for r in range(count):
        m = jnp.max(lev[0], axis=0, keepdims=True)
        out.append(m)
        hit = lev[0] == m
        for k in range(min(g, count - r - 1)):
            lev[k] = jnp.where(hit, lev[k + 1], lev[k])
    if not with_rank:
        return out
    rank = jnp.zeros(x.shape, F32)
    for v in out:
        rank = rank + jnp.where(v > x, 1.0, 0.0)
    return out, rank


_PAIRS = [(a, b) for a in range(PEER_TOPK) for b in range(PEER_TOPK) if (a + 1) * (b + 1) <= PEER_TOPK]
_CAND_ROWS = 8 << (-(-len(_PAIRS) // 8) - 1).bit_length()


_HI16 = 0xFFFF0000


def _bf16_hi_bits(x):
    u = lax.bitcast_convert_type(x, jnp.uint32)
    u = u + jnp.uint32(0x7FFF) + ((u >> 16) & jnp.uint32(1))
    return u & jnp.uint32(_HI16)


def _pack_pair(a, b):
    return _bf16_hi_bits(a) | (_bf16_hi_bits(b) >> 16)


def _sum_pair(w):
    return (lax.bitcast_convert_type(w & jnp.uint32(_HI16), F32)
            + lax.bitcast_convert_type(w << 16, F32))


def _head_tables(s1, s2, cand_s, tie_safe):
    k = PEER_TOPK
    lanes = s1.shape[-1]
    if tie_safe:
        v1 = _extract_top(s1, k)
        v2, rank2 = _extract_top(s2, k, with_rank=True)
    else:
        v1 = _top_distinct(s1, k)
        v2, rank2 = _top_distinct(s2, k, with_rank=True)
    cand_s[...] = jnp.full((_CAND_ROWS, lanes), -jnp.inf, F32)
    for r, (a, b) in enumerate(_PAIRS):
        cand_s[r:r + 1, :] = v1[a] + v2[b]
    cand = cand_s[...]
    top = _extract_top(cand, k) if tie_safe else _top_distinct(cand, k)
    thr = top[k - 1]
    z = jnp.zeros_like(thr)
    for tk in top:
        z = z + jnp.exp(tk - top[0])
    count = lambda cond: jnp.sum(jnp.where(cond, 1.0, 0.0), axis=0, keepdims=True)
    if tie_safe:
        n1 = jnp.zeros_like(s1)
        for b in range(k):
            n1 = n1 + jnp.where(s1 + v2[b] >= thr, 1.0, 0.0)
        bad = None
    else:
        n1 = jnp.zeros_like(s1)
        for b in range(4):
            n1 = n1 + jnp.where(s1 + v2[b] >= thr, 1.0, 0.0)
        for a in range(3):
            n_a = jnp.zeros_like(thr)
            for b in range(k // (a + 1)):
                n_a = n_a + jnp.where(v1[a] + v2[b] >= thr, 1.0, 0.0)
            n1 = jnp.where(s1 == v1[a], n_a, n1)
        bad = ((count(s1 >= v1[k - 1]) != k) | (count(s2 >= v2[k - 1]) != k)
               | (count(cand >= thr) != k))
    return (rank2, jnp.exp(s2 - v2[0]), n1, jnp.exp(s1 - v1[0]) / z), bad


def _topk_kernel(st_ref, r2_ref, e2_ref, n1_ref, f1_ref, cand_s):
    refs = (r2_ref, e2_ref, n1_ref, f1_ref)
    for hp in range(PEER_HEADS // 2):
        def tables(tie_safe):
            ta, bad_a = _head_tables(st_ref[4 * hp], st_ref[4 * hp + 1], cand_s, tie_safe)
            tb, bad_b = _head_tables(st_ref[4 * hp + 2], st_ref[4 * hp + 3], cand_s, tie_safe)
            for ref, a, b in zip(refs, ta, tb):
                ref[hp] = _pack_pair(a, b)
            return None if tie_safe else jnp.max(jnp.where(bad_a | bad_b, 1.0, 0.0))

        any_bad = tables(tie_safe=False)

        @pl.when(any_bad > 0.0)
        def _():
            tables(tie_safe=True)


def _peer_topk(st, tt=256):
    t = st.shape[-1]
    tab = pl.BlockSpec((PEER_HEADS // 2, PEER_N_KEYS, tt), lambda i: (0, 0, i))
    out = jax.ShapeDtypeStruct((PEER_HEADS // 2, PEER_N_KEYS, t), jnp.uint32)
    return pl.pallas_call(
        _topk_kernel,
        grid=(t // tt,),
        in_specs=[pl.BlockSpec((2 * PEER_HEADS, PEER_N_KEYS, tt), lambda i: (0, 0, i))],
        out_specs=[tab] * 4,
        out_shape=[out] * 4,
        scratch_shapes=[pltpu.VMEM((_CAND_ROWS, tt), F32)],
        compiler_params=_params("parallel"),
        name="peer_topk",
    )(st)


_GELU_C0 = math.sqrt(2.0 / math.pi)
_GELU_C1 = 0.044715 * _GELU_C0


def _gelu_tanh(x):
    hx = 0.5 * x
    return hx + hx * jnp.tanh(x * (_GELU_C0 + _GELU_C1 * (x * x)))


def _peer_kernel(h2_ref, u_ref, vt_ref, r2_ref, e2_ref, n1_ref, f1_ref, x1_ref, x2_ref,
                 acc_s, ut_s, wt_s, *, ec, lw):
    j = pl.program_id(1)
    tt = h2_ref.shape[0]
    zero = jnp.zeros((), BF16)

    @pl.when(j == 0)
    def _():
        acc_s[...] = jnp.zeros_like(acc_s)

    ut_s[...] = _nt(pltpu.bitcast(u_ref[...], BF16), h2_ref[...])
    for r0 in range(0, ec, PEER_N_KEYS):
        q = r0 // PEER_N_KEYS
        grp = j * (ec // PEER_N_KEYS // 8) + q // 8
        sub = slice(q % 8, q % 8 + 1)
        for c0 in range(0, tt, lw):
            cs = slice(c0, c0 + lw)
            gate = None
            for hp in range(PEER_HEADS // 2):
                row = lambda ref: pltpu.bitcast(
                    jnp.broadcast_to(ref[hp, grp, sub, cs], (PEER_N_KEYS, lw)), BF16)
                r2 = pltpu.bitcast(r2_ref[hp, :, cs], BF16)
                e2 = pltpu.bitcast(e2_ref[hp, :, cs], BF16)
                term = jnp.where(r2 < row(n1_ref), e2 * row(f1_ref), zero)
                gate = term if gate is None else gate + term
            g = _sum_pair(pltpu.bitcast(gate, jnp.uint32))
            act = _gelu_tanh(ut_s[r0:r0 + PEER_N_KEYS, cs])
            wt_s[r0:r0 + PEER_N_KEYS, cs] = (g * act).astype(BF16)
    acc_s[...] += _mm(pltpu.bitcast(vt_ref[...], BF16), wt_s[...])

    @pl.when(j == pl.num_programs(1) - 1)
    def _():
        x2_ref[...] = x1_ref[...] + acc_s[...].T


def _peer_dense(h2, u, vt, r2, e2, n1, f1, x1):
    t = h2.shape[0]
    tt, ec = PEER_TT, PEER_EC
    n_exp = 2 * u.shape[0]
    tab = pl.BlockSpec((PEER_HEADS // 2, PEER_N_KEYS, tt), lambda i, j: (0, 0, i))
    tab1 = pl.BlockSpec((PEER_HEADS // 2, PEER_N_KEYS // 8, 8, tt), lambda i, j: (0, 0, 0, i))
    n1 = n1.reshape(PEER_HEADS // 2, PEER_N_KEYS // 8, 8, t)
    f1 = f1.reshape(PEER_HEADS // 2, PEER_N_KEYS // 8, 8, t)
    return pl.pallas_call(
        functools.partial(_peer_kernel, ec=ec, lw=PEER_LW),
        grid=(t // tt, n_exp // ec),
        in_specs=[
            pl.BlockSpec((tt, D_MODEL), lambda i, j: (i, 0)),
            pl.BlockSpec((ec // 2, D_MODEL), lambda i, j: (j, 0)),
            pl.BlockSpec((D_MODEL // 2, ec), lambda i, j: (0, j)),
            tab, tab, tab1, tab1,
            pl.BlockSpec((tt, D_MODEL), lambda i, j: (i, 0)),
        ],
        out_specs=pl.BlockSpec((tt, D_MODEL), lambda i, j: (i, 0)),
        out_shape=jax.ShapeDtypeStruct((t, D_MODEL), F32),
        scratch_shapes=[pltpu.VMEM((D_MODEL, tt), F32),
                        pltpu.VMEM((ec, tt), F32),
                        pltpu.VMEM((ec, tt), BF16)],
        compiler_params=_params("parallel", "arbitrary"),
        name="peer_dense",
    )(h2, u, vt, r2, e2, n1, f1, x1)


def _ple_kernel(x_ref, p_ref, g3_ref, wg_ref, wp_ref, gf_ref, y_ref, *, final_norm):
    x = x_ref[...]
    h3 = _rms(x, g3_ref[...]).astype(BF16)
    gate = jax.nn.sigmoid(_mm(h3, wg_ref[...]))
    x3 = x + _mm(p_ref[...].astype(BF16), wp_ref[...]) * gate
    y_ref[...] = _rms(x3, gf_ref[...]) if final_norm else x3


def _ple(x, p, g3, wg, wp, gf, final_norm, tm=512):
    t = x.shape[0]
    row = lambda c: pl.BlockSpec((tm, c), lambda i: (i, 0))
    return pl.pallas_call(
        functools.partial(_ple_kernel, final_norm=final_norm),
        grid=(t // tm,),
        in_specs=[row(D_MODEL), row(PLE_DIM)] + [_const_spec(a.shape) for a in (g3, wg, wp, gf)],
        out_specs=row(D_MODEL),
        out_shape=jax.ShapeDtypeStruct((t, D_MODEL), F32),
        compiler_params=_params("parallel"),
        name="ple",
    )(x, p, g3, wg, wp, gf)


def _t5_bucket(rel):
    nb = REL_BUCKETS // 2
    max_exact = nb // 2
    ret = (rel > 0).astype(jnp.int32) * nb
    n = jnp.abs(rel)
    nf = jnp.maximum(n, 1).astype(jnp.float32)
    large = max_exact + (jnp.log(nf / max_exact) / math.log(REL_MAX_DIST / max_exact)
                         * (nb - max_exact)).astype(jnp.int32)
    large = jnp.minimum(large, nb - 1)
    return ret + jnp.where(n < max_exact, n, large)


def _bias_table(rel_bias, tq):
    e = jnp.arange(5, dtype=jnp.int32)[:, None, None]
    r = jnp.arange(tq, dtype=jnp.int32)[None, :, None]
    c = jnp.arange(tq, dtype=jnp.int32)[None, None, :]
    bucket = _t5_bucket(r - c + (e - 2) * tq)[None]
    rb = rel_bias.astype(F32) * LOG2E
    tab = jnp.zeros((rb.shape[1], 5, tq, tq), F32)
    for k in range(REL_BUCKETS):
        tab = jnp.where(bucket == k, rb[k][:, None, None, None], tab)
    return tab


def _pack_row_pairs(w):
    bits = lax.bitcast_convert_type(w.astype(BF16), jnp.uint16).astype(jnp.uint32)
    bits = bits.reshape(w.shape[0] // 2, 2, w.shape[1])
    return bits[:, 0] | (bits[:, 1] << 16)


def _pack_in_proj(w_in, alpha_w, alpha_b):
    qa, ka, va, qg, kg, vg, rg, lr, gl = jnp.split(
        w_in, [512, 1024, 1536, 1792, 2048, 2560, 3072, 3104], axis=1)
    dup = lambda w: jnp.concatenate([w.reshape(-1, GLA_HEADS, 1, GLA_DK)] * 2, axis=2).reshape(-1, 512)
    lr_pad = jnp.pad(lr, ((0, 0), (0, LANES - lr.shape[1])))
    w = jnp.concatenate([qa * (DA_HEAD_DIM ** -0.5 * LOG2E), ka, dup(qg) * GLA_DK ** -0.5, dup(kg),
                         vg, rg, gl, lr_pad], axis=1).astype(BF16)
    aw = jnp.zeros((LANES, GLA_HEADS, 2, GLA_DK), F32)
    for d in range(2):
        aw = aw.at[d * GLA_GATE_RANK:(d + 1) * GLA_GATE_RANK, :, d, :].set(
            alpha_w[d].reshape(GLA_GATE_RANK, GLA_HEADS, GLA_DK))
    ab = alpha_b.reshape(2, GLA_HEADS, GLA_DK).transpose(1, 0, 2).reshape(1, 512)
    return w, va.T.astype(BF16), aw.reshape(LANES, 512).astype(BF16), ab.astype(F32)


def _trunk(x, p, rel_bias, norm1_g, w_in, lambda_qk, da_norm_g, gla_alpha_w, gla_alpha_b, gla_norm_g,
           w_up_a, w_up_b, w_out, norm2_g, peer_w_q, peer_sub_keys, peer_u, peer_v, norm3_g,
           ple_w, ple_gate_w, final_norm_g, prepared):
    batch, seq, _ = x.shape
    depth = w_in.shape[0]
    xf = x.reshape(batch * seq, D_MODEL)
    row = lambda v: v.reshape(1, -1).astype(F32)
    for i in range(depth):
        w1, wvt, aw, ab, bias_tab, u_bf, vt_bf, sk = prepared[i]
        lam_init = 0.8 - 0.6 * math.exp(-0.3 * i)
        qa, ka, vat, qg, kg, vg, rgs, gates, la = _in_proj(xf, row(norm1_g[i]), w1, wvt, aw, ab)
        oa = _diff_attn(qa, ka, vat, bias_tab, lambda_qk[i].astype(F32), row(da_norm_g[i]),
                        batch, seq, lam_init)
        og = _gla(qg, kg, vg, la, rgs, row(gla_norm_g[i]), batch, seq)
        x1, h2, st = _mix(oa, og, gates, xf, w_up_a[i].astype(BF16), w_up_b[i].astype(BF16),
                          w_out[i].astype(BF16), row(norm2_g[i]), peer_w_q[i].astype(BF16), sk)
        r2, e2, n1, f1 = _peer_topk(st)
        x2 = _peer_dense(h2, u_bf, vt_bf, r2, e2, n1, f1, x1)
        xf = _ple(x2, p[i].reshape(batch * seq, PLE_DIM), row(norm3_g[i]), ple_gate_w[i].astype(BF16),
                  ple_w[i].astype(BF16), row(final_norm_g), final_norm=(i == depth - 1))
    return xf.reshape(batch, seq, D_MODEL)


def kernel(x_prompt, x_sample, p_prompt, p_sample, rel_bias, norm1_g, w_in, lambda_qk, da_norm_g,
           gla_alpha_w, gla_alpha_b, gla_norm_g, w_up_a, w_up_b, w_out, norm2_g, peer_w_q,
           peer_sub_keys, peer_u, peer_v, norm3_g, ple_w, ple_gate_w, final_norm_g):
    prepared = []
    for i in range(w_in.shape[0]):
        w1, wvt, aw, ab = _pack_in_proj(w_in[i], gla_alpha_w[i], gla_alpha_b[i])
        prepared.append((w1, wvt, aw, ab, _bias_table(rel_bias, ATT_TQ), _pack_row_pairs(peer_u[i]),
                         _pack_row_pairs(peer_v[i].T),
                         peer_sub_keys[i].reshape(2 * PEER_HEADS, PEER_N_KEYS, -1).astype(BF16)))
    shared = (rel_bias, norm1_g, w_in, lambda_qk, da_norm_g, gla_alpha_w, gla_alpha_b, gla_norm_g,
              w_up_a, w_up_b, w_out, norm2_g, peer_w_q, peer_sub_keys, peer_u, peer_v, norm3_g,
              ple_w, ple_gate_w, final_norm_g)
    y_prompt = _trunk(x_prompt, p_prompt, *shared, prepared)
    y_sample = _trunk(x_sample, p_sample, *shared, prepared)
    return (y_prompt, y_sample)
```

```python
import functools
import math

import jax
import jax.numpy as jnp
from jax import lax
from jax.experimental import pallas as pl
from jax.experimental.pallas import tpu as pltpu

F32 = jnp.float32
BF16 = jnp.bfloat16

D_MODEL = 1024
DA_HEADS = 4
DA_HEAD_DIM = 64
DA_V_DIM = 128
GLA_HEADS = 4
GLA_DK = 64
GLA_DV = 128
GLA_GATE_RANK = 16
GLA_GATE_TAU = 16.0
GLA_CHUNK = 64
REL_BUCKETS = 32
REL_MAX_DIST = 128
PEER_HEADS = 8
PEER_N_KEYS = 128
PEER_TOPK = 16
PLE_DIM = 256
EPS = 1e-6

LANES = 128
V7X_VMEM_BYTES = 64 * 1024 * 1024
VMEM_LIMIT = V7X_VMEM_BYTES - 8 * 1024 * 1024

_C_QA, _C_KA, _C_QG, _C_KG, _C_VG, _C_RG, _C_GL, _C_LR, _C_END = (
    0, 512, 1024, 1536, 2048, 2560, 3072, 5120, 5248)

GLA_UNROLL = 16
GLA_GROUP = 4
ATT_TQ = 256
IN_PROJ_TM = 512
LOG2E = math.log2(math.e)
PEER_TT = 512
PEER_EC = 2048
PEER_LW = 128


def _nt(a, b):
    return lax.dot_general(a, b, (((1,), (1,)), ((), ())), preferred_element_type=F32)


def _tn(a, b):
    return lax.dot_general(a, b, (((0,), (0,)), ((), ())), preferred_element_type=F32)


def _mm(a, b):
    return jnp.dot(a, b, preferred_element_type=F32)


def _rms(x, g):
    return x * lax.rsqrt(jnp.mean(x * x, axis=-1, keepdims=True) + EPS) * g


def _params(*sem):
    return pltpu.CompilerParams(dimension_semantics=sem, vmem_limit_bytes=VMEM_LIMIT)


def _const_spec(shape):
    nd = len(shape)
    return pl.BlockSpec(shape, lambda *_: (0,) * nd)


def _in_proj_kernel(x_ref, g_ref, w_ref, wvt_ref, aw_ref, ab_ref,
                    qa_ref, ka_ref, vat_ref, qg_ref, kg_ref, vg_ref, rg_ref, gate_ref, la_ref):
    h = _rms(x_ref[...], g_ref[...]).astype(BF16)

    def proj(lo, hi):
        return _mm(h, w_ref[:, lo:hi])

    qa_ref[...] = proj(_C_QA, _C_KA).astype(BF16)
    ka_ref[...] = proj(_C_KA, _C_QG).astype(BF16)
    vat = _nt(wvt_ref[...], h).astype(BF16)
    for hd in range(DA_HEADS):
        for c in range(vat_ref.shape[1]):
            vat_ref[hd, c] = vat[hd * DA_V_DIM:(hd + 1) * DA_V_DIM, c * ATT_TQ:(c + 1) * ATT_TQ]
    qg_ref[...] = proj(_C_QG, _C_KG)
    kg_ref[...] = proj(_C_KG, _C_VG)
    vg_ref[...] = proj(_C_VG, _C_RG)
    rg = proj(_C_RG, _C_GL)
    rg_ref[...] = rg * jax.nn.sigmoid(rg)
    gate_ref[...] = jax.nn.sigmoid(proj(_C_GL, _C_LR)).astype(BF16)
    lr = proj(_C_LR, _C_END).astype(BF16)
    z = _mm(lr, aw_ref[...]) + ab_ref[...]
    la_ref[...] = (jnp.minimum(z, 0.0) - jnp.log1p(jnp.exp(-jnp.abs(z)))) * (1.0 / GLA_GATE_TAU)


def _in_proj(x, g, w, wvt, aw, ab):
    t = x.shape[0]
    tm = IN_PROJ_TM
    kc = tm // ATT_TQ
    row = lambda c: pl.BlockSpec((tm, c), lambda i: (i, 0))
    outs = [(512, BF16)] * 2 + [None] + [(512, F32)] * 4 + [(2048, BF16), (512, F32)]
    vat_spec = pl.BlockSpec((DA_HEADS, kc, DA_V_DIM, ATT_TQ), lambda i: (0, i, 0, 0))
    vat_shape = jax.ShapeDtypeStruct((DA_HEADS, t // ATT_TQ, DA_V_DIM, ATT_TQ), BF16)
    return pl.pallas_call(
        _in_proj_kernel,
        grid=(t // tm,),
        in_specs=[row(D_MODEL), _const_spec(g.shape), _const_spec(w.shape), _const_spec(wvt.shape),
                  _const_spec(aw.shape), _const_spec(ab.shape)],
        out_specs=[vat_spec if o is None else row(o[0]) for o in outs],
        out_shape=[vat_shape if o is None else jax.ShapeDtypeStruct((t, o[0]), o[1]) for o in outs],
        compiler_params=_params("parallel"),
        name="in_proj",
    )(x, g, w, wvt, aw, ab)


def _attn_kernel(q_ref, k_ref, vt_ref, bias_ref, lq_ref, g_ref, o_ref,
                 qb_s, s0_s, s1_s, p0_s, p1_s, a0_s, a1_s, m_s, l_s, acc_s, *, tq, nk, lam_init):
    i = pl.program_id(2)
    q = q_ref[...]
    lane = lax.broadcasted_iota(jnp.int32, q.shape, 1)
    zero = jnp.zeros_like(q)
    qb_s[0:tq, :] = jnp.where(lane < DA_HEAD_DIM, q, zero)
    qb_s[tq:2 * tq, :] = jnp.where(lane >= DA_HEAD_DIM, q, zero)
    m_s[...] = jnp.full(m_s.shape, -jnp.inf, F32)
    l_s[...] = jnp.zeros(l_s.shape, F32)
    acc_s[...] = jnp.zeros(acc_s.shape, F32)
    s_bufs, p_bufs, a_bufs = (s0_s, s1_s), (p0_s, p1_s), (a0_s, a1_s)

    def scores(t, buf):
        kc = k_ref[pl.ds(pl.multiple_of(t * tq, tq), tq), :]
        bias = bias_ref[jnp.clip(t - i, -2, 2) + 2]
        s = _nt(kc, qb_s[...])
        s_bufs[buf][:, 0:tq] = s[:, 0:tq] + bias
        s_bufs[buf][:, tq:2 * tq] = s[:, tq:2 * tq] + bias

    def softmax(buf):
        for c0 in range(0, 2 * tq, LANES):
            cs = slice(c0, c0 + LANES)
            s = s_bufs[buf][:, cs]
            m_old = m_s[:, cs]
            m_new = jnp.maximum(m_old, jnp.max(s, axis=0, keepdims=True))
            alpha = jnp.exp2(m_old - m_new)
            p = jnp.exp2(s - m_new)
            l_s[:, cs] = alpha * l_s[:, cs] + jnp.sum(p, axis=0, keepdims=True)
            m_s[:, cs] = m_new
            a_bufs[buf][:, cs] = alpha
            p_bufs[buf][:, cs] = p.astype(BF16)

    def values(t, buf):
        acc_s[...] = a_bufs[buf][...] * acc_s[...] + _mm(vt_ref[t], p_bufs[buf][...])

    scores(0, 0)
    scores(1, 1)
    softmax(0)

    def steady(u, carry):
        t = 2 + 2 * u
        scores(t, 0)
        softmax(1)
        values(t - 2, 0)
        scores(t + 1, 1)
        softmax(0)
        values(t - 1, 1)
        return carry

    lax.fori_loop(0, (nk - 2) // 2, steady, 0)
    softmax(1)
    values(nk - 2, 0)
    values(nk - 1, 1)

    lq = lq_ref[...]
    lam = (jnp.exp(jnp.sum(lq[0:1] * lq[1:2], axis=-1, keepdims=True))
           - jnp.exp(jnp.sum(lq[2:3] * lq[3:4], axis=-1, keepdims=True)) + lam_init)
    ot = acc_s[:, 0:tq] / l_s[:, 0:tq] - lam * (acc_s[:, tq:2 * tq] / l_s[:, tq:2 * tq])
    o_ref[...] = (_rms(ot.T, g_ref[...]) * (1.0 - lam_init)).astype(BF16)


def _diff_attn(qa, ka, vat, bias_tab, lq, g, batch, seq, lam_init):
    tq = ATT_TQ
    nq = seq // tq
    assert nq % 2 == 0 and nq >= 2
    stat = pltpu.VMEM((1, 2 * tq), F32)
    return pl.pallas_call(
        functools.partial(_attn_kernel, tq=tq, nk=nq, lam_init=lam_init),
        grid=(batch, DA_HEADS, nq),
        in_specs=[
            pl.BlockSpec((tq, LANES), lambda b, h, i: (b * nq + i, h)),
            pl.BlockSpec((seq, LANES), lambda b, h, i: (b, h)),
            pl.BlockSpec((None, nq, DA_V_DIM, tq), lambda b, h, i: (h, b, 0, 0)),
            pl.BlockSpec((None, 5, tq, tq), lambda b, h, i: (h, 0, 0, 0)),
            _const_spec(lq.shape),
            _const_spec(g.shape),
        ],
        out_specs=pl.BlockSpec((tq, LANES), lambda b, h, i: (b * nq + i, h)),
        out_shape=jax.ShapeDtypeStruct(qa.shape, BF16),
        scratch_shapes=[pltpu.VMEM((2 * tq, LANES), BF16),
                        pltpu.VMEM((tq, 2 * tq), F32), pltpu.VMEM((tq, 2 * tq), F32),
                        pltpu.VMEM((tq, 2 * tq), BF16), pltpu.VMEM((tq, 2 * tq), BF16),
                        stat, stat, stat, stat,
                        pltpu.VMEM((DA_V_DIM, 2 * tq), F32)],
        compiler_params=_params("parallel", "parallel", "arbitrary"),
        name="diff_attn",
    )(qa, ka, vat, bias_tab, lq, g)


def _split3(x):
    hi = x.astype(BF16)
    r = x - hi.astype(F32)
    mid = r.astype(BF16)
    lo = (r - mid.astype(F32)).astype(BF16)
    return hi, mid, lo


def _gla_kernel(q_ref, k_ref, v_ref, la_ref, rg_ref, g_ref, o_ref,
                qe_s, kv_s, dec_s, oacc_s, *, nc, rows):
    c = GLA_CHUNK
    gc = GLA_GROUP
    gr = gc * c
    shift = c.bit_length() - 1
    r_i = lax.broadcasted_iota(jnp.int32, (gr, gr), 0)
    c_i = lax.broadcasted_iota(jnp.int32, (gr, gr), 1)
    same_chunk = jnp.right_shift(r_i, shift) == jnp.right_shift(c_i, shift)
    pre = jnp.where(same_chunk & (c_i <= r_i), 1.0, 0.0).astype(BF16)
    suf = jnp.where(same_chunk & (c_i >= r_i), 1.0, 0.0).astype(BF16)
    causal = (lax.broadcasted_iota(jnp.int32, (c, c), 1)
              <= lax.broadcasted_iota(jnp.int32, (c, c), 0))
    fwd_lane_g = lax.broadcasted_iota(jnp.int32, (gr, LANES), 1) < GLA_DK
    fwd_lane = lax.broadcasted_iota(jnp.int32, (c, LANES), 1) < GLA_DK

    def group(gi, _):
        off = pl.multiple_of(gi * gr, gr)
        la = la_ref[pl.ds(off, gr), :]
        b = jnp.zeros((gr, LANES), F32)
        for part in _split3(jnp.where(fwd_lane_g, la, 0.0)):
            b = b + _mm(pre, part)
        for part in _split3(jnp.where(fwd_lane_g, 0.0, la)):
            b = b + _mm(suf, part)
        k = k_ref[pl.ds(off, gr), :]
        qe = q_ref[pl.ds(off, gr), :] * jnp.exp(b)
        qe_s[pl.ds(off, gr), :] = qe
        qeb = qe.astype(BF16)
        keb = (k * jnp.exp(-b)).astype(BF16)
        vb = v_ref[pl.ds(off, gr), :].astype(BF16)
        zero = jnp.zeros((c, LANES), BF16)
        for ci in range(gc):
            rs = slice(ci * c, (ci + 1) * c)
            b_last = jnp.where(fwd_lane[0:1], b[rs][c - 1:c], b[rs][0:1])
            kdb = (k[rs] * jnp.exp(b_last - b[rs])).astype(BF16)
            stacked = jnp.concatenate([jnp.where(fwd_lane, qeb[rs], zero),
                                       jnp.where(fwd_lane, zero, qeb[rs])], axis=0)
            a2 = _nt(stacked, keb[rs])
            att = jnp.where(causal, a2[0:c], a2[c:2 * c])
            oacc_s[pl.ds(off + ci * c, c), :] = _mm(att.astype(BF16), vb[rs])
            n = gi * gc + ci
            kv_s[n] = _tn(vb[rs], kdb)
            dec_s[n] = jnp.broadcast_to(jnp.exp(b_last), (8, LANES))
        return 0

    lax.fori_loop(0, nc // gc, group, 0, unroll=GLA_UNROLL // gc)

    def sweep(forward):
        def step(t, state):
            n = t if forward else nc - 1 - t
            off = pl.multiple_of(n * c, c)
            qe = qe_s[pl.ds(off, c), :]
            qm = jnp.where(fwd_lane, qe, 0.0) if forward else jnp.where(fwd_lane, 0.0, qe)
            oacc_s[pl.ds(off, c), :] += _nt(qm.astype(BF16), state.astype(BF16))
            return state * dec_s[n][0:1] + kv_s[n]
        lax.fori_loop(0, nc, step, jnp.zeros((GLA_DV, LANES), F32), unroll=GLA_UNROLL)

    sweep(True)
    sweep(False)

    def fin(r, _):
        off = pl.multiple_of(r * rows, rows)
        o = oacc_s[pl.ds(off, rows), :]
        o_ref[pl.ds(off, rows), :] = (_rms(o, g_ref[...]) * rg_ref[pl.ds(off, rows), :]).astype(BF16)
        return 0

    lax.fori_loop(0, (nc * c) // rows, fin, 0)


def _gla(qg, kg, vg, la, rgs, g, batch, seq):
    nc = seq // GLA_CHUNK
    blk = pl.BlockSpec((seq, LANES), lambda b, h: (b, h))
    return pl.pallas_call(
        functools.partial(_gla_kernel, nc=nc, rows=256),
        grid=(batch, GLA_HEADS),
        in_specs=[blk, blk, blk, blk, blk, _const_spec(g.shape)],
        out_specs=blk,
        out_shape=jax.ShapeDtypeStruct(vg.shape, BF16),
        scratch_shapes=[
            pltpu.VMEM((seq, LANES), F32),
            pltpu.VMEM((nc, GLA_DV, LANES), F32),
            pltpu.VMEM((nc, 8, LANES), F32),
            pltpu.VMEM((seq, LANES), F32),
        ],
        compiler_params=_params("parallel", "parallel"),
        name="gla",
    )(qg, kg, vg, la, rgs, g)


def _mix_kernel(oa_ref, og_ref, gate_ref, x_ref, wa_ref, wb_ref, wo_ref, g2_ref, wq_ref, sk_ref,
                x1_ref, h2_ref, st_ref):
    ya = _mm(oa_ref[...], wa_ref[...])
    yb = _mm(og_ref[...], wb_ref[...])
    merged = gate_ref[:, :D_MODEL].astype(F32) * ya + gate_ref[:, D_MODEL:].astype(F32) * yb
    x1 = x_ref[...] + _mm(merged.astype(BF16), wo_ref[...])
    x1_ref[...] = x1
    h2 = _rms(x1, g2_ref[...]).astype(BF16)
    h2_ref[...] = h2
    q = _mm(h2, wq_ref[...]).astype(BF16)
    for hc in range(2 * PEER_HEADS):
        st_ref[hc] = _nt(sk_ref[hc], q[:, hc * LANES:(hc + 1) * LANES])


def _mix(oa, og, gates, x, wa, wb, wo, g2, wq, sk, tm=512):
    t = x.shape[0]
    row = lambda c: pl.BlockSpec((tm, c), lambda i: (i, 0))
    return pl.pallas_call(
        _mix_kernel,
        grid=(t // tm,),
        in_specs=[row(512), row(512), row(2048), row(D_MODEL)]
        + [_const_spec(a.shape) for a in (wa, wb, wo, g2, wq, sk)],
        out_specs=[row(D_MODEL), row(D_MODEL),
                   pl.BlockSpec((2 * PEER_HEADS, PEER_N_KEYS, tm), lambda i: (0, 0, i))],
        out_shape=[jax.ShapeDtypeStruct((t, D_MODEL), F32),
                   jax.ShapeDtypeStruct((t, D_MODEL), BF16),
                   jax.ShapeDtypeStruct((2 * PEER_HEADS, PEER_N_KEYS, t), F32)],
        compiler_params=_params("parallel"),
        name="mix",
    )(oa, og, gates, x, wa, wb, wo, g2, wq, sk)


def _extract_top(cur, count, with_rank=False):
    rows = lax.broadcasted_iota(jnp.int32, cur.shape, 0)
    rank = jnp.full(cur.shape, float(count), F32)
    out = []
    for r in range(count):
        m = jnp.max(cur, axis=0, keepdims=True)
        out.append(m)
        hit = rows == jnp.min(jnp.where(cur == m, rows, cur.shape[0]), axis=0, keepdims=True)
        cur = jnp.where(hit, -jnp.inf, cur)
        if with_rank:
            rank = jnp.where(hit, float(r), rank)
    return (out, rank) if with_rank else out


def _oddeven_merge_sort_pairs(n):
    pairs = []
    p = 1
    while p < n:
        k = p
        while k >= 1:
            for j in range(k % p, n - k, 2 * k):
                for i in range(min(k, n - j - k)):
                    if (i + j) // (2 * p) == (i + j + k) // (2 * p):
                        pairs.append((i + j, i + j + k))
            k //= 2
        p *= 2
    return pairs


def _top_distinct(x, count, with_rank=False):
    g = x.shape[0] // 8
    lev = [x[8 * i:8 * (i + 1)] for i in range(g)]
    for i, j in _oddeven_merge_sort_pairs(g):
        lev[i], lev[j] = jnp.maximum(lev[i], lev[j]), jnp.minimum(lev[i], lev[j])
    lev.append(jnp.full_like(lev[0], -jnp.inf))
    out = []
    for r in range(count):
        m = jnp.max(lev[0], axis=0, keepdims=True)
        out.append(m)
        hit = lev[0] == m
        for k in range(min(g, count - r - 1)):
            lev[k] = jnp.where(hit, lev[k + 1], lev[k])
    if not with_rank:
        return out
    rank = jnp.zeros(x.shape, F32)
    for v in out:
        rank = rank + jnp.where(v > x, 1.0, 0.0)
    return out, rank


_PAIRS = [(a, b) for a in range(PEER_TOPK) for b in range(PEER_TOPK) if (a + 1) * (b + 1) <= PEER_TOPK]
_CAND_ROWS = 8 << (-(-len(_PAIRS) // 8) - 1).bit_length()


_HI16 = 0xFFFF0000


def _bf16_hi_bits(x):
    u = lax.bitcast_convert_type(x, jnp.uint32)
    u = u + jnp.uint32(0x7FFF) + ((u >> 16) & jnp.uint32(1))
    return u & jnp.uint32(_HI16)


def _pack_pair(a, b):
    return _bf16_hi_bits(a) | (_bf16_hi_bits(b) >> 16)


def _sum_pair(w):
    return (lax.bitcast_convert_type(w & jnp.uint32(_HI16), F32)
            + lax.bitcast_convert_type(w << 16, F32))


def _head_tables(s1, s2, cand_s, tie_safe):
    k = PEER_TOPK
    lanes = s1.shape[-1]
    if tie_safe:
        v1 = _extract_top(s1, k)
        v2, rank2 = _extract_top(s2, k, with_rank=True)
    else:
        v1 = _top_distinct(s1, k)
        v2, rank2 = _top_distinct(s2, k, with_rank=True)
    cand_s[...] = jnp.full((_CAND_ROWS, lanes), -jnp.inf, F32)
    for r, (a, b) in enumerate(_PAIRS):
        cand_s[r:r + 1, :] = v1[a] + v2[b]
    cand = cand_s[...]
    top = _extract_top(cand, k) if tie_safe else _top_distinct(cand, k)
    thr = top[k - 1]
    z = jnp.zeros_like(thr)
    for tk in top:
        z = z + jnp.exp(tk - top[0])
    count = lambda cond: jnp.sum(jnp.where(cond, 1.0, 0.0), axis=0, keepdims=True)
    if tie_safe:
        n1 = jnp.zeros_like(s1)
        for b in range(k):
            n1 = n1 + jnp.where(s1 + v2[b] >= thr, 1.0, 0.0)
        bad = None
    else:
        n1 = jnp.zeros_like(s1)
        for b in range(4):
            n1 = n1 + jnp.where(s1 + v2[b] >= thr, 1.0, 0.0)
        for a in range(3):
            n_a = jnp.zeros_like(thr)
            for b in range(k // (a + 1)):
                n_a = n_a + jnp.where(v1[a] + v2[b] >= thr, 1.0, 0.0)
            n1 = jnp.where(s1 == v1[a], n_a, n1)
        bad = ((count(s1 >= v1[k - 1]) != k) | (count(s2 >= v2[k - 1]) != k)
               | (count(cand >= thr) != k))
    return (rank2, jnp.exp(s2 - v2[0]), n1, jnp.exp(s1 - v1[0]) / z), bad


def _topk_kernel(st_ref, r2_ref, e2_ref, n1_ref, f1_ref, cand_s):
    refs = (r2_ref, e2_ref, n1_ref, f1_ref)
    for hp in range(PEER_HEADS // 2):
        def tables(tie_safe):
            ta, bad_a = _head_tables(st_ref[4 * hp], st_ref[4 * hp + 1], cand_s, tie_safe)
            tb, bad_b = _head_tables(st_ref[4 * hp + 2], st_ref[4 * hp + 3], cand_s, tie_safe)
            for ref, a, b in zip(refs, ta, tb):
                ref[hp] = _pack_pair(a, b)
            return None if tie_safe else jnp.max(jnp.where(bad_a | bad_b, 1.0, 0.0))

        any_bad = tables(tie_safe=False)

        @pl.when(any_bad > 0.0)
        def _():
            tables(tie_safe=True)


def _peer_topk(st, tt=256):
    t = st.shape[-1]
    tab = pl.BlockSpec((PEER_HEADS // 2, PEER_N_KEYS, tt), lambda i: (0, 0, i))
    out = jax.ShapeDtypeStruct((PEER_HEADS // 2, PEER_N_KEYS, t), jnp.uint32)
    return pl.pallas_call(
        _topk_kernel,
        grid=(t // tt,),
        in_specs=[pl.BlockSpec((2 * PEER_HEADS, PEER_N_KEYS, tt), lambda i: (0, 0, i))],
        out_specs=[tab] * 4,
        out_shape=[out] * 4,
        scratch_shapes=[pltpu.VMEM((_CAND_ROWS, tt), F32)],
        compiler_params=_params("parallel"),
        name="peer_topk",
    )(st)


_GELU_C0 = math.sqrt(2.0 / math.pi)
_GELU_C1 = 0.044715 * _GELU_C0


def _gelu_tanh(x):
    hx = 0.5 * x
    return hx + hx * jnp.tanh(x * (_GELU_C0 + _GELU_C1 * (x * x)))


def _peer_kernel(h2_ref, u_ref, vt_ref, r2_ref, e2_ref, n1_ref, f1_ref, x1_ref, x2_ref,
                 acc_s, ut_s, wt_s, *, ec, lw):
    j = pl.program_id(1)
    tt = h2_ref.shape[0]
    zero = jnp.zeros((), BF16)

    @pl.when(j == 0)
    def _():
        acc_s[...] = jnp.zeros_like(acc_s)

    ut_s[...] = _nt(pltpu.bitcast(u_ref[...], BF16), h2_ref[...])
    for r0 in range(0, ec, PEER_N_KEYS):
        q = r0 // PEER_N_KEYS
        grp = j * (ec // PEER_N_KEYS // 8) + q // 8
        sub = slice(q % 8, q % 8 + 1)
        for c0 in range(0, tt, lw):
            cs = slice(c0, c0 + lw)
            gate = None
            for hp in range(PEER_HEADS // 2):
                row = lambda ref: pltpu.bitcast(
                    jnp.broadcast_to(ref[hp, grp, sub, cs], (PEER_N_KEYS, lw)), BF16)
                r2 = pltpu.bitcast(r2_ref[hp, :, cs], BF16)
                e2 = pltpu.bitcast(e2_ref[hp, :, cs], BF16)
                term = jnp.where(r2 < row(n1_ref), e2 * row(f1_ref), zero)
                gate = term if gate is None else gate + term
            g = _sum_pair(pltpu.bitcast(gate, jnp.uint32))
            act = _gelu_tanh(ut_s[r0:r0 + PEER_N_KEYS, cs])
            wt_s[r0:r0 + PEER_N_KEYS, cs] = (g * act).astype(BF16)
    acc_s[...] += _mm(pltpu.bitcast(vt_ref[...], BF16), wt_s[...])

    @pl.when(j == pl.num_programs(1) - 1)
    def _():
        x2_ref[...] = x1_ref[...] + acc_s[...].T


def _peer_dense(h2, u, vt, r2, e2, n1, f1, x1):
    t = h2.shape[0]
    tt, ec = PEER_TT, PEER_EC
    n_exp = 2 * u.shape[0]
    tab = pl.BlockSpec((PEER_HEADS // 2, PEER_N_KEYS, tt), lambda i, j: (0, 0, i))
    tab1 = pl.BlockSpec((PEER_HEADS // 2, PEER_N_KEYS // 8, 8, tt), lambda i, j: (0, 0, 0, i))
    n1 = n1.reshape(PEER_HEADS // 2, PEER_N_KEYS // 8, 8, t)
    f1 = f1.reshape(PEER_HEADS // 2, PEER_N_KEYS // 8, 8, t)
    return pl.pallas_call(
        functools.partial(_peer_kernel, ec=ec, lw=PEER_LW),
        grid=(t // tt, n_exp // ec),
        in_specs=[
            pl.BlockSpec((tt, D_MODEL), lambda i, j: (i, 0)),
            pl.BlockSpec((ec // 2, D_MODEL), lambda i, j: (j, 0)),
            pl.BlockSpec((D_MODEL // 2, ec), lambda i, j: (0, j)),
            tab, tab, tab1, tab1,
            pl.BlockSpec((tt, D_MODEL), lambda i, j: (i, 0)),
        ],
        out_specs=pl.BlockSpec((tt, D_MODEL), lambda i, j: (i, 0)),
        out_shape=jax.ShapeDtypeStruct((t, D_MODEL), F32),
        scratch_shapes=[pltpu.VMEM((D_MODEL, tt), F32),
                        pltpu.VMEM((ec, tt), F32),
                        pltpu.VMEM((ec, tt), BF16)],
        compiler_params=_params("parallel", "arbitrary"),
        name="peer_dense",
    )(h2, u, vt, r2, e2, n1, f1, x1)


def _ple_kernel(x_ref, p_ref, g3_ref, wg_ref, wp_ref, gf_ref, y_ref, *, final_norm):
    x = x_ref[...]
    h3 = _rms(x, g3_ref[...]).astype(BF16)
    gate = jax.nn.sigmoid(_mm(h3, wg_ref[...]))
    x3 = x + _mm(p_ref[...].astype(BF16), wp_ref[...]) * gate
    y_ref[...] = _rms(x3, gf_ref[...]) if final_norm else x3


def _ple(x, p, g3, wg, wp, gf, final_norm, tm=512):
    t = x.shape[0]
    row = lambda c: pl.BlockSpec((tm, c), lambda i: (i, 0))
    return pl.pallas_call(
        functools.partial(_ple_kernel, final_norm=final_norm),
        grid=(t // tm,),
        in_specs=[row(D_MODEL), row(PLE_DIM)] + [_const_spec(a.shape) for a in (g3, wg, wp, gf)],
        out_specs=row(D_MODEL),
        out_shape=jax.ShapeDtypeStruct((t, D_MODEL), F32),
        compiler_params=_params("parallel"),
        name="ple",
    )(x, p, g3, wg, wp, gf)


def _t5_bucket(rel):
    nb = REL_BUCKETS // 2
    max_exact = nb // 2
    ret = (rel > 0).astype(jnp.int32) * nb
    n = jnp.abs(rel)
    nf = jnp.maximum(n, 1).astype(jnp.float32)
    large = max_exact + (jnp.log(nf / max_exact) / math.log(REL_MAX_DIST / max_exact)
                         * (nb - max_exact)).astype(jnp.int32)
    large = jnp.minimum(large, nb - 1)
    return ret + jnp.where(n < max_exact, n, large)


def _bias_table(rel_bias, tq):
    e = jnp.arange(5, dtype=jnp.int32)[:, None, None]
    r = jnp.arange(tq, dtype=jnp.int32)[None, :, None]
    c = jnp.arange(tq, dtype=jnp.int32)[None, None, :]
    bucket = _t5_bucket(r - c + (e - 2) * tq)[None]
    rb = rel_bias.astype(F32) * LOG2E
    tab = jnp.zeros((rb.shape[1], 5, tq, tq), F32)
    for k in range(REL_BUCKETS):
        tab = jnp.where(bucket == k, rb[k][:, None, None, None], tab)
    return tab


def _pack_row_pairs(w):
    bits = lax.bitcast_convert_type(w.astype(BF16), jnp.uint16).astype(jnp.uint32)
    bits = bits.reshape(w.shape[0] // 2, 2, w.shape[1])
    return bits[:, 0] | (bits[:, 1] << 16)


def _pack_in_proj(w_in, alpha_w, alpha_b):
    qa, ka, va, qg, kg, vg, rg, lr, gl = jnp.split(
        w_in, [512, 1024, 1536, 1792, 2048, 2560, 3072, 3104], axis=1)
    dup = lambda w: jnp.concatenate([w.reshape(-1, GLA_HEADS, 1, GLA_DK)] * 2, axis=2).reshape(-1, 512)
    lr_pad = jnp.pad(lr, ((0, 0), (0, LANES - lr.shape[1])))
    w = jnp.concatenate([qa * (DA_HEAD_DIM ** -0.5 * LOG2E), ka, dup(qg) * GLA_DK ** -0.5, dup(kg),
                         vg, rg, gl, lr_pad], axis=1).astype(BF16)
    aw = jnp.zeros((LANES, GLA_HEADS, 2, GLA_DK), F32)
    for d in range(2):
        aw = aw.at[d * GLA_GATE_RANK:(d + 1) * GLA_GATE_RANK, :, d, :].set(
            alpha_w[d].reshape(GLA_GATE_RANK, GLA_HEADS, GLA_DK))
    ab = alpha_b.reshape(2, GLA_HEADS, GLA_DK).transpose(1, 0, 2).reshape(1, 512)
    return w, va.T.astype(BF16), aw.reshape(LANES, 512).astype(BF16), ab.astype(F32)


def _trunk(x, p, rel_bias, norm1_g, w_in, lambda_qk, da_norm_g, gla_alpha_w, gla_alpha_b, gla_norm_g,
           w_up_a, w_up_b, w_out, norm2_g, peer_w_q, peer_sub_keys, peer_u, peer_v, norm3_g,
           ple_w, ple_gate_w, final_norm_g, prepared):
    batch, seq, _ = x.shape
    depth = w_in.shape[0]
    xf = x.reshape(batch * seq, D_MODEL)
    row = lambda v: v.reshape(1, -1).astype(F32)
    for i in range(depth):
        w1, wvt, aw, ab, bias_tab, u_bf, vt_bf, sk = prepared[i]
        lam_init = 0.8 - 0.6 * math.exp(-0.3 * i)
        qa, ka, vat, qg, kg, vg, rgs, gates, la = _in_proj(xf, row(norm1_g[i]), w1, wvt, aw, ab)
        oa = _diff_attn(qa, ka, vat, bias_tab, lambda_qk[i].astype(F32), row(da_norm_g[i]),
                        batch, seq, lam_init)
        og = _gla(qg, kg, vg, la, rgs, row(gla_norm_g[i]), batch, seq)
        x1, h2, st = _mix(oa, og, gates, xf, w_up_a[i].astype(BF16), w_up_b[i].astype(BF16),
                          w_out[i].astype(BF16), row(norm2_g[i]), peer_w_q[i].astype(BF16), sk)
        r2, e2, n1, f1 = _peer_topk(st)
        x2 = _peer_dense(h2, u_bf, vt_bf, r2, e2, n1, f1, x1)
        xf = _ple(x2, p[i].reshape(batch * seq, PLE_DIM), row(norm3_g[i]), ple_gate_w[i].astype(BF16),
                  ple_w[i].astype(BF16), row(final_norm_g), final_norm=(i == depth - 1))
    return xf.reshape(batch, seq, D_MODEL)


def kernel(x_prompt, x_sample, p_prompt, p_sample, rel_bias, norm1_g, w_in, lambda_qk, da_norm_g,
           gla_alpha_w, gla_alpha_b, gla_norm_g, w_up_a, w_up_b, w_out, norm2_g, peer_w_q,
           peer_sub_keys, peer_u, peer_v, norm3_g, ple_w, ple_gate_w, final_norm_g):
    prepared = []
    for i in range(w_in.shape[0]):
        w1, wvt, aw, ab = _pack_in_proj(w_in[i], gla_alpha_w[i], gla_alpha_b[i])
        prepared.append((w1, wvt, aw, ab, _bias_table(rel_bias, ATT_TQ), _pack_row_pairs(peer_u[i]),
                         _pack_row_pairs(peer_v[i].T),
                         peer_sub_keys[i].reshape(2 * PEER_HEADS, PEER_N_KEYS, -1).astype(BF16)))
    shared = (rel_bias, norm1_g, w_in, lambda_qk, da_norm_g, gla_alpha_w, gla_alpha_b, gla_norm_g,
              w_up_a, w_up_b, w_out, norm2_g, peer_w_q, peer_sub_keys, peer_u, peer_v, norm3_g,
              ple_w, ple_gate_w, final_norm_g)
    y_prompt = _trunk(x_prompt, p_prompt, *shared, prepared)
    y_sample = _trunk(x_sample, p_sample, *shared, prepared)
    return (y_prompt, y_sample)
```

```python
import functools
import math

import jax
import jax.numpy as jnp
from jax import lax
from jax.experimental import pallas as pl
from jax.experimental.pallas import tpu as pltpu

F32 = jnp.float32
BF16 = jnp.bfloat16

D_MODEL = 1024
DA_HEADS = 4
DA_HEAD_DIM = 64
DA_V_DIM = 128
GLA_HEADS = 4
GLA_DK = 64
GLA_DV = 128
GLA_GATE_RANK = 16
GLA_GATE_TAU = 16.0
GLA_CHUNK = 64
REL_BUCKETS = 32
REL_MAX_DIST = 128
PEER_HEADS = 8
PEER_N_KEYS = 128
PEER_TOPK = 16
PLE_DIM = 256
EPS = 1e-6

LANES = 128
V7X_VMEM_BYTES = 64 * 1024 * 1024
VMEM_LIMIT = V7X_VMEM_BYTES - 8 * 1024 * 1024

_C_QA, _C_KA, _C_QG, _C_KG, _C_VG, _C_RG, _C_GL, _C_LR, _C_END = (
    0, 512, 1024, 1536, 2048, 2560, 3072, 5120, 5248)

GLA_UNROLL = 16
GLA_GROUP = 4
ATT_TQ = 256
ATT_NT = 2
IN_PROJ_TM = 512
LOG2E = math.log2(math.e)
PEER_TT = 512
PEER_EC = 2048
PEER_LW = 128


def _nt(a, b):
    return lax.dot_general(a, b, (((1,), (1,)), ((), ())), preferred_element_type=F32)


def _tn(a, b):
    return lax.dot_general(a, b, (((0,), (0,)), ((), ())), preferred_element_type=F32)


def _mm(a, b):
    return jnp.dot(a, b, preferred_element_type=F32)


def _rms(x, g):
    return x * lax.rsqrt(jnp.mean(x * x, axis=-1, keepdims=True) + EPS) * g


def _params(*sem):
    return pltpu.CompilerParams(dimension_semantics=sem, vmem_limit_bytes=VMEM_LIMIT)


def _const_spec(shape):
    nd = len(shape)
    return pl.BlockSpec(shape, lambda *_: (0,) * nd)


def _in_proj_kernel(x_ref, g_ref, w_ref, wvt_ref, aw_ref, ab_ref,
                    qa_ref, ka_ref, vat_ref, qg_ref, kg_ref, vg_ref, rg_ref, gate_ref, la_ref):
    h = _rms(x_ref[...], g_ref[...]).astype(BF16)

    def proj(lo, hi):
        return _mm(h, w_ref[:, lo:hi])

    qa_ref[...] = proj(_C_QA, _C_KA).astype(BF16)
    ka_ref[...] = proj(_C_KA, _C_QG).astype(BF16)
    vat = _nt(wvt_ref[...], h).astype(BF16)
    for hd in range(DA_HEADS):
        for c in range(vat_ref.shape[1]):
            vat_ref[hd, c] = vat[hd * DA_V_DIM:(hd + 1) * DA_V_DIM, c * ATT_TQ:(c + 1) * ATT_TQ]
    qg_ref[...] = proj(_C_QG, _C_KG)
    kg_ref[...] = proj(_C_KG, _C_VG)
    vg_ref[...] = proj(_C_VG, _C_RG)
    rg = proj(_C_RG, _C_GL)
    rg_ref[...] = rg * jax.nn.sigmoid(rg)
    gate_ref[...] = jax.nn.sigmoid(proj(_C_GL, _C_LR)).astype(BF16)
    lr = proj(_C_LR, _C_END).astype(BF16)
    z = _mm(lr, aw_ref[...]) + ab_ref[...]
    la_ref[...] = (jnp.minimum(z, 0.0) - jnp.log1p(jnp.exp(-jnp.abs(z)))) * (1.0 / GLA_GATE_TAU)


def _in_proj(x, g, w, wvt, aw, ab):
    t = x.shape[0]
    tm = IN_PROJ_TM
    kc = tm // ATT_TQ
    row = lambda c: pl.BlockSpec((tm, c), lambda i: (i, 0))
    outs = [(512, BF16)] * 2 + [None] + [(512, F32)] * 4 + [(2048, BF16), (512, F32)]
    vat_spec = pl.BlockSpec((DA_HEADS, kc, DA_V_DIM, ATT_TQ), lambda i: (0, i, 0, 0))
    vat_shape = jax.ShapeDtypeStruct((DA_HEADS, t // ATT_TQ, DA_V_DIM, ATT_TQ), BF16)
    return pl.pallas_call(
        _in_proj_kernel,
        grid=(t // tm,),
        in_specs=[row(D_MODEL), _const_spec(g.shape), _const_spec(w.shape), _const_spec(wvt.shape),
                  _const_spec(aw.shape), _const_spec(ab.shape)],
        out_specs=[vat_spec if o is None else row(o[0]) for o in outs],
        out_shape=[vat_shape if o is None else jax.ShapeDtypeStruct((t, o[0]), o[1]) for o in outs],
        compiler_params=_params("parallel"),
        name="in_proj",
    )(x, g, w, wvt, aw, ab)


def _attn_kernel(q_ref, k_ref, vt_ref, bias_ref, lq_ref, g_ref, o_ref,
                 qb_s, s0_s, s1_s, p0_s, p1_s, a0_s, a1_s, m_s, l_s, acc_s, *, tq, nk, nt, lam_init):
    i = pl.program_id(2)
    w = 2 * tq
    for a in range(nt):
        q = q_ref[a * tq:(a + 1) * tq, :]
        lane = lax.broadcasted_iota(jnp.int32, q.shape, 1)
        zero = jnp.zeros_like(q)
        qb_s[a * w:a * w + tq, :] = jnp.where(lane < DA_HEAD_DIM, q, zero)
        qb_s[a * w + tq:(a + 1) * w, :] = jnp.where(lane >= DA_HEAD_DIM, q, zero)
    m_s[...] = jnp.full(m_s.shape, -jnp.inf, F32)
    l_s[...] = jnp.zeros(l_s.shape, F32)
    acc_s[...] = jnp.zeros(acc_s.shape, F32)
    s_bufs, p_bufs, a_bufs = (s0_s, s1_s), (p0_s, p1_s), (a0_s, a1_s)

    def scores(t, buf):
        kc = k_ref[pl.ds(pl.multiple_of(t * tq, tq), tq), :]
        s = _nt(kc, qb_s[...])
        for a in range(nt):
            bias = bias_ref[jnp.clip(t - (i * nt + a), -2, 2) + 2]
            s_bufs[buf][:, a * w:a * w + tq] = s[:, a * w:a * w + tq] + bias
            s_bufs[buf][:, a * w + tq:(a + 1) * w] = s[:, a * w + tq:(a + 1) * w] + bias

    def softmax(buf):
        for c0 in range(0, nt * w, LANES):
            cs = slice(c0, c0 + LANES)
            s = s_bufs[buf][:, cs]
            m_old = m_s[:, cs]
            m_new = jnp.maximum(m_old, jnp.max(s, axis=0, keepdims=True))
            alpha = jnp.exp2(m_old - m_new)
            p = jnp.exp2(s - m_new)
            l_s[:, cs] = alpha * l_s[:, cs] + jnp.sum(p, axis=0, keepdims=True)
            m_s[:, cs] = m_new
            a_bufs[buf][:, cs] = alpha
            p_bufs[buf][:, cs] = p.astype(BF16)

    def values(t, buf):
        acc_s[...] = a_bufs[buf][...] * acc_s[...] + _mm(vt_ref[t], p_bufs[buf][...])

    scores(0, 0)
    scores(1, 1)
    softmax(0)

    def steady(u, carry):
        t = 2 + 2 * u
        scores(t, 0)
        softmax(1)
        values(t - 2, 0)
        scores(t + 1, 1)
        softmax(0)
        values(t - 1, 1)
        return carry

    lax.fori_loop(0, (nk - 2) // 2, steady, 0)
    softmax(1)
    values(nk - 2, 0)
    values(nk - 1, 1)

    lq = lq_ref[...]
    lam = (jnp.exp(jnp.sum(lq[0:1] * lq[1:2], axis=-1, keepdims=True))
           - jnp.exp(jnp.sum(lq[2:3] * lq[3:4], axis=-1, keepdims=True)) + lam_init)
    for a in range(nt):
        c1 = slice(a * w, a * w + tq)
        c2 = slice(a * w + tq, (a + 1) * w)
        ot = acc_s[:, c1] / l_s[:, c1] - lam * (acc_s[:, c2] / l_s[:, c2])
        o_ref[a * tq:(a + 1) * tq, :] = (_rms(ot.T, g_ref[...]) * (1.0 - lam_init)).astype(BF16)


def _diff_attn(qa, ka, vat, bias_tab, lq, g, batch, seq, lam_init):
    tq, nt = ATT_TQ, ATT_NT
    nk = seq // tq
    nq = nk // nt
    assert nk % 2 == 0 and nk >= 2 and nk % nt == 0
    stat = pltpu.VMEM((1, nt * 2 * tq), F32)
    return pl.pallas_call(
        functools.partial(_attn_kernel, tq=tq, nk=nk, nt=nt, lam_init=lam_init),
        grid=(batch, DA_HEADS, nq),
        in_specs=[
            pl.BlockSpec((nt * tq, LANES), lambda b, h, i: (b * nq + i, h)),
            pl.BlockSpec((seq, LANES), lambda b, h, i: (b, h)),
            pl.BlockSpec((None, nk, DA_V_DIM, tq), lambda b, h, i: (h, b, 0, 0)),
            pl.BlockSpec((None, 5, tq, tq), lambda b, h, i: (h, 0, 0, 0)),
            _const_spec(lq.shape),
            _const_spec(g.shape),
        ],
        out_specs=pl.BlockSpec((nt * tq, LANES), lambda b, h, i: (b * nq + i, h)),
        out_shape=jax.ShapeDtypeStruct(qa.shape, BF16),
        scratch_shapes=[pltpu.VMEM((nt * 2 * tq, LANES), BF16),
                        pltpu.VMEM((tq, nt * 2 * tq), F32), pltpu.VMEM((tq, nt * 2 * tq), F32),
                        pltpu.VMEM((tq, nt * 2 * tq), BF16), pltpu.VMEM((tq, nt * 2 * tq), BF16),
                        stat, stat, stat, stat,
                        pltpu.VMEM((DA_V_DIM, nt * 2 * tq), F32)],
        compiler_params=_params("parallel", "parallel", "arbitrary"),
        name="diff_attn",
    )(qa, ka, vat, bias_tab, lq, g)


def _split3(x):
    hi = x.astype(BF16)
    r = x - hi.astype(F32)
    mid = r.astype(BF16)
    lo = (r - mid.astype(F32)).astype(BF16)
    return hi, mid, lo


def _gla_kernel(q_ref, k_ref, v_ref, la_ref, rg_ref, g_ref, o_ref,
                qe_s, kv_s, dec_s, oacc_s, *, nc, rows):
    c = GLA_CHUNK
    gc = GLA_GROUP
    gr = gc * c
    shift = c.bit_length() - 1
    r_i = lax.broadcasted_iota(jnp.int32, (gr, gr), 0)
    c_i = lax.broadcasted_iota(jnp.int32, (gr, gr), 1)
    same_chunk = jnp.right_shift(r_i, shift) == jnp.right_shift(c_i, shift)
    pre = jnp.where(same_chunk & (c_i <= r_i), 1.0, 0.0).astype(BF16)
    suf = jnp.where(same_chunk & (c_i >= r_i), 1.0, 0.0).astype(BF16)
    causal = (lax.broadcasted_iota(jnp.int32, (c, c), 1)
              <= lax.broadcasted_iota(jnp.int32, (c, c), 0))
    fwd_lane_g = lax.broadcasted_iota(jnp.int32, (gr, LANES), 1) < GLA_DK
    fwd_lane = lax.broadcasted_iota(jnp.int32, (c, LANES), 1) < GLA_DK

    def group(gi, _):
        off = pl.multiple_of(gi * gr, gr)
        la = la_ref[pl.ds(off, gr), :]
        b = jnp.zeros((gr, LANES), F32)
        for part in _split3(jnp.where(fwd_lane_g, la, 0.0)):
            b = b + _mm(pre, part)
        for part in _split3(jnp.where(fwd_lane_g, 0.0, la)):
            b = b + _mm(suf, part)
        k = k_ref[pl.ds(off, gr), :]
        qe = q_ref[pl.ds(off, gr), :] * jnp.exp(b)
        qe_s[pl.ds(off, gr), :] = qe
        qeb = qe.astype(BF16)
        keb = (k * jnp.exp(-b)).astype(BF16)
        vb = v_ref[pl.ds(off, gr), :].astype(BF16)
        zero = jnp.zeros((c, LANES), BF16)
        for ci in range(gc):
            rs = slice(ci * c, (ci + 1) * c)
            b_last = jnp.where(fwd_lane[0:1], b[rs][c - 1:c], b[rs][0:1])
            kdb = (k[rs] * jnp.exp(b_last - b[rs])).astype(BF16)
            stacked = jnp.concatenate([jnp.where(fwd_lane, qeb[rs], zero),
                                       jnp.where(fwd_lane, zero, qeb[rs])], axis=0)
            a2 = _nt(stacked, keb[rs])
            att = jnp.where(causal, a2[0:c], a2[c:2 * c])
            oacc_s[pl.ds(off + ci * c, c), :] = _mm(att.astype(BF16), vb[rs])
            n = gi * gc + ci
            kv_s[n] = _tn(vb[rs], kdb)
            dec_s[n] = jnp.broadcast_to(jnp.exp(b_last), (8, LANES))
        return 0

    lax.fori_loop(0, nc // gc, group, 0, unroll=GLA_UNROLL // gc)

    def sweep(forward):
        def step(t, state):
            n = t if forward else nc - 1 - t
            off = pl.multiple_of(n * c, c)
            qe = qe_s[pl.ds(off, c), :]
            qm = jnp.where(fwd_lane, qe, 0.0) if forward else jnp.where(fwd_lane, 0.0, qe)
            oacc_s[pl.ds(off, c), :] += _nt(qm.astype(BF16), state.astype(BF16))
            return state * dec_s[n][0:1] + kv_s[n]
        lax.fori_loop(0, nc, step, jnp.zeros((GLA_DV, LANES), F32), unroll=GLA_UNROLL)

    sweep(True)
    sweep(False)

    def fin(r, _):
        off = pl.multiple_of(r * rows, rows)
        o = oacc_s[pl.ds(off, rows), :]
        o_ref[pl.ds(off, rows), :] = (_rms(o, g_ref[...]) * rg_ref[pl.ds(off, rows), :]).astype(BF16)
        return 0

    lax.fori_loop(0, (nc * c) // rows, fin, 0)


def _gla(qg, kg, vg, la, rgs, g, batch, seq):
    nc = seq // GLA_CHUNK
    blk = pl.BlockSpec((seq, LANES), lambda b, h: (b, h))
    return pl.pallas_call(
        functools.partial(_gla_kernel, nc=nc, rows=256),
        grid=(batch, GLA_HEADS),
        in_specs=[blk, blk, blk, blk, blk, _const_spec(g.shape)],
        out_specs=blk,
        out_shape=jax.ShapeDtypeStruct(vg.shape, BF16),
        scratch_shapes=[
            pltpu.VMEM((seq, LANES), F32),
            pltpu.VMEM((nc, GLA_DV, LANES), F32),
            pltpu.VMEM((nc, 8, LANES), F32),
            pltpu.VMEM((seq, LANES), F32),
        ],
        compiler_params=_params("parallel", "parallel"),
        name="gla",
    )(qg, kg, vg, la, rgs, g)


def _mix_kernel(oa_ref, og_ref, gate_ref, x_ref, wa_ref, wb_ref, wo_ref, g2_ref, wq_ref, sk_ref,
                x1_ref, h2_ref, st_ref):
    ya = _mm(oa_ref[...], wa_ref[...])
    yb = _mm(og_ref[...], wb_ref[...])
    merged = gate_ref[:, :D_MODEL].astype(F32) * ya + gate_ref[:, D_MODEL:].astype(F32) * yb
    x1 = x_ref[...] + _mm(merged.astype(BF16), wo_ref[...])
    x1_ref[...] = x1
    h2 = _rms(x1, g2_ref[...]).astype(BF16)
    h2_ref[...] = h2
    q = _mm(h2, wq_ref[...]).astype(BF16)
    for hc in range(2 * PEER_HEADS):
        st_ref[hc] = _nt(sk_ref[hc], q[:, hc * LANES:(hc + 1) * LANES])


def _mix(oa, og, gates, x, wa, wb, wo, g2, wq, sk, tm=512):
    t = x.shape[0]
    row = lambda c: pl.BlockSpec((tm, c), lambda i: (i, 0))
    return pl.pallas_call(
        _mix_kernel,
        grid=(t // tm,),
        in_specs=[row(512), row(512), row(2048), row(D_MODEL)]
        + [_const_spec(a.shape) for a in (wa, wb, wo, g2, wq, sk)],
        out_specs=[row(D_MODEL), row(D_MODEL),
                   pl.BlockSpec((2 * PEER_HEADS, PEER_N_KEYS, tm), lambda i: (0, 0, i))],
        out_shape=[jax.ShapeDtypeStruct((t, D_MODEL), F32),
                   jax.ShapeDtypeStruct((t, D_MODEL), BF16),
                   jax.ShapeDtypeStruct((2 * PEER_HEADS, PEER_N_KEYS, t), F32)],
        compiler_params=_params("parallel"),
        name="mix",
    )(oa, og, gates, x, wa, wb, wo, g2, wq, sk)


def _extract_top(cur, count, with_rank=False):
    rows = lax.broadcasted_iota(jnp.int32, cur.shape, 0)
    rank = jnp.full(cur.shape, float(count), F32)
    out = []
    for r in range(count):
        m = jnp.max(cur, axis=0, keepdims=True)
        out.append(m)
        hit = rows == jnp.min(jnp.where(cur == m, rows, cur.shape[0]), axis=0, keepdims=True)
        cur = jnp.where(hit, -jnp.inf, cur)
        if with_rank:
            rank = jnp.where(hit, float(r), rank)
    return (out, rank) if with_rank else out


def _oddeven_merge_sort_pairs(n):
    pairs = []
    p = 1
    while p < n:
        k = p
        while k >= 1:
            for j in range(k % p, n - k, 2 * k):
                for i in range(min(k, n - j - k)):
                    if (i + j) // (2 * p) == (i + j + k) // (2 * p):
                        pairs.append((i + j, i + j + k))
            k //= 2
        p *= 2
    return pairs


def _top_distinct(x, count, with_rank=False):
    g = x.shape[0] // 8
    lev = [x[8 * i:8 * (i + 1)] for i in range(g)]
    for i, j in _oddeven_merge_sort_pairs(g):
        lev[i], lev[j] = jnp.maximum(lev[i], lev[j]), jnp.minimum(lev[i], lev[j])
    lev.append(jnp.full_like(lev[0], -jnp.inf))
    out = []
    for r in range(count):
        m = jnp.max(lev[0], axis=0, keepdims=True)
        out.append(m)
        hit = lev[0] == m
        for k in range(min(g, count - r - 1)):
            lev[k] = jnp.where(hit, lev[k + 1], lev[k])
    if not with_rank:
        return out
    rank = jnp.zeros(x.shape, F32)
    for v in out:
        rank = rank + jnp.where(v > x, 1.0, 0.0)
    return out, rank


_PAIRS = [(a, b) for a in range(PEER_TOPK) for b in range(PEER_TOPK) if (a + 1) * (b + 1) <= PEER_TOPK]
_CAND_ROWS = 8 << (-(-len(_PAIRS) // 8) - 1).bit_length()


_HI16 = 0xFFFF0000


def _bf16_hi_bits(x):
    u = lax.bitcast_convert_type(x, jnp.uint32)
    u = u + jnp.uint32(0x7FFF) + ((u >> 16) & jnp.uint32(1))
    return u & jnp.uint32(_HI16)


def _pack_pair(a, b):
    return _bf16_hi_bits(a) | (_bf16_hi_bits(b) >> 16)


def _sum_pair(w):
    return (lax.bitcast_convert_type(w & jnp.uint32(_HI16), F32)
            + lax.bitcast_convert_type(w << 16, F32))


def _head_tables(s1, s2, cand_s, tie_safe):
    k = PEER_TOPK
    lanes = s1.shape[-1]
    if tie_safe:
        v1 = _extract_top(s1, k)
        v2, rank2 = _extract_top(s2, k, with_rank=True)
    else:
        v1 = _top_distinct(s1, k)
        v2, rank2 = _top_distinct(s2, k, with_rank=True)
    cand_s[...] = jnp.full((_CAND_ROWS, lanes), -jnp.inf, F32)
    for r, (a, b) in enumerate(_PAIRS):
        cand_s[r:r + 1, :] = v1[a] + v2[b]
    cand = cand_s[...]
    top = _extract_top(cand, k) if tie_safe else _top_distinct(cand, k)
    thr = top[k - 1]
    z = jnp.zeros_like(thr)
    for tk in top:
        z = z + jnp.exp(tk - top[0])
    count = lambda cond: jnp.sum(jnp.where(cond, 1.0, 0.0), axis=0, keepdims=True)
    if tie_safe:
        n1 = jnp.zeros_like(s1)
        for b in range(k):
            n1 = n1 + jnp.where(s1 + v2[b] >= thr, 1.0, 0.0)
        bad = None
    else:
        n1 = jnp.zeros_like(s1)
        for b in range(4):
            n1 = n1 + jnp.where(s1 + v2[b] >= thr, 1.0, 0.0)
        for a in range(3):
            n_a = jnp.zeros_like(thr)
            for b in range(k // (a + 1)):
                n_a = n_a + jnp.where(v1[a] + v2[b] >= thr, 1.0, 0.0)
            n1 = jnp.where(s1 == v1[a], n_a, n1)
        bad = ((count(s1 >= v1[k - 1]) != k) | (count(s2 >= v2[k - 1]) != k)
               | (count(cand >= thr) != k))
    return (rank2, jnp.exp(s2 - v2[0]), n1, jnp.exp(s1 - v1[0]) / z), bad


def _topk_kernel(st_ref, r2_ref, e2_ref, n1_ref, f1_ref, cand_s):
    refs = (r2_ref, e2_ref, n1_ref, f1_ref)
    for hp in range(PEER_HEADS // 2):
        def tables(tie_safe):
            ta, bad_a = _head_tables(st_ref[4 * hp], st_ref[4 * hp + 1], cand_s, tie_safe)
            tb, bad_b = _head_tables(st_ref[4 * hp + 2], st_ref[4 * hp + 3], cand_s, tie_safe)
            for ref, a, b in zip(refs, ta, tb):
                ref[hp] = _pack_pair(a, b)
            return None if tie_safe else jnp.max(jnp.where(bad_a | bad_b, 1.0, 0.0))

        any_bad = tables(tie_safe=False)

        @pl.when(any_bad > 0.0)
        def _():
            tables(tie_safe=True)


def _peer_topk(st, tt=256):
    t = st.shape[-1]
    tab = pl.BlockSpec((PEER_HEADS // 2, PEER_N_KEYS, tt), lambda i: (0, 0, i))
    out = jax.ShapeDtypeStruct((PEER_HEADS // 2, PEER_N_KEYS, t), jnp.uint32)
    return pl.pallas_call(
        _topk_kernel,
        grid=(t // tt,),
        in_specs=[pl.BlockSpec((2 * PEER_HEADS, PEER_N_KEYS, tt), lambda i: (0, 0, i))],
        out_specs=[tab] * 4,
        out_shape=[out] * 4,
        scratch_shapes=[pltpu.VMEM((_CAND_ROWS, tt), F32)],
        compiler_params=_params("parallel"),
        name="peer_topk",
    )(st)


_GELU_C0 = math.sqrt(2.0 / math.pi)
_GELU_C1 = 0.044715 * _GELU_C0


def _gelu_tanh(x):
    hx = 0.5 * x
    return hx + hx * jnp.tanh(x * (_GELU_C0 + _GELU_C1 * (x * x)))


def _peer_kernel(h2_ref, u_ref, vt_ref, r2_ref, e2_ref, n1_ref, f1_ref, x1_ref, x2_ref,
                 acc_s, ut_s, wt_s, *, ec, lw):
    j = pl.program_id(1)
    tt = h2_ref.shape[0]
    zero = jnp.zeros((), BF16)

    @pl.when(j == 0)
    def _():
        acc_s[...] = jnp.zeros_like(acc_s)

    ut_s[...] = _nt(pltpu.bitcast(u_ref[...], BF16), h2_ref[...])
    for r0 in range(0, ec, PEER_N_KEYS):
        q = r0 // PEER_N_KEYS
        grp = j * (ec // PEER_N_KEYS // 8) + q // 8
        sub = slice(q % 8, q % 8 + 1)
        for c0 in range(0, tt, lw):
            cs = slice(c0, c0 + lw)
            gate = None
            for hp in range(PEER_HEADS // 2):
                row = lambda ref: pltpu.bitcast(
                    jnp.broadcast_to(ref[hp, grp, sub, cs], (PEER_N_KEYS, lw)), BF16)
                r2 = pltpu.bitcast(r2_ref[hp, :, cs], BF16)
                e2 = pltpu.bitcast(e2_ref[hp, :, cs], BF16)
                term = jnp.where(r2 < row(n1_ref), e2 * row(f1_ref), zero)
                gate = term if gate is None else gate + term
            g = _sum_pair(pltpu.bitcast(gate, jnp.uint32))
            act = _gelu_tanh(ut_s[r0:r0 + PEER_N_KEYS, cs])
            wt_s[r0:r0 + PEER_N_KEYS, cs] = (g * act).astype(BF16)
    acc_s[...] += _mm(pltpu.bitcast(vt_ref[...], BF16), wt_s[...])

    @pl.when(j == pl.num_programs(1) - 1)
    def _():
        x2_ref[...] = x1_ref[...] + acc_s[...].T


def _peer_dense(h2, u, vt, r2, e2, n1, f1, x1):
    t = h2.shape[0]
    tt, ec = PEER_TT, PEER_EC
    n_exp = 2 * u.shape[0]
    tab = pl.BlockSpec((PEER_HEADS // 2, PEER_N_KEYS, tt), lambda i, j: (0, 0, i))
    tab1 = pl.BlockSpec((PEER_HEADS // 2, PEER_N_KEYS // 8, 8, tt), lambda i, j: (0, 0, 0, i))
    n1 = n1.reshape(PEER_HEADS // 2, PEER_N_KEYS // 8, 8, t)
    f1 = f1.reshape(PEER_HEADS // 2, PEER_N_KEYS // 8, 8, t)
    return pl.pallas_call(
        functools.partial(_peer_kernel, ec=ec, lw=PEER_LW),
        grid=(t // tt, n_exp // ec),
        in_specs=[
            pl.BlockSpec((tt, D_MODEL), lambda i, j: (i, 0)),
            pl.BlockSpec((ec // 2, D_MODEL), lambda i, j: (j, 0)),
            pl.BlockSpec((D_MODEL // 2, ec), lambda i, j: (0, j)),
            tab, tab, tab1, tab1,
            pl.BlockSpec((tt, D_MODEL), lambda i, j: (i, 0)),
        ],
        out_specs=pl.BlockSpec((tt, D_MODEL), lambda i, j: (i, 0)),
        out_shape=jax.ShapeDtypeStruct((t, D_MODEL), F32),
        scratch_shapes=[pltpu.VMEM((D_MODEL, tt), F32),
                        pltpu.VMEM((ec, tt), F32),
                        pltpu.VMEM((ec, tt), BF16)],
        compiler_params=_params("parallel", "arbitrary"),
        name="peer_dense",
    )(h2, u, vt, r2, e2, n1, f1, x1)


def _ple_kernel(x_ref, p_ref, g3_ref, wg_ref, wp_ref, gf_ref, y_ref, *, final_norm):
    x = x_ref[...]
    h3 = _rms(x, g3_ref[...]).astype(BF16)
    gate = jax.nn.sigmoid(_mm(h3, wg_ref[...]))
    x3 = x + _mm(p_ref[...].astype(BF16), wp_ref[...]) * gate
    y_ref[...] = _rms(x3, gf_ref[...]) if final_norm else x3


def _ple(x, p, g3, wg, wp, gf, final_norm, tm=512):
    t = x.shape[0]
    row = lambda c: pl.BlockSpec((tm, c), lambda i: (i, 0))
    return pl.pallas_call(
        functools.partial(_ple_kernel, final_norm=final_norm),
        grid=(t // tm,),
        in_specs=[row(D_MODEL), row(PLE_DIM)] + [_const_spec(a.shape) for a in (g3, wg, wp, gf)],
        out_specs=row(D_MODEL),
        out_shape=jax.ShapeDtypeStruct((t, D_MODEL), F32),
        compiler_params=_params("parallel"),
        name="ple",
    )(x, p, g3, wg, wp, gf)


def _t5_bucket(rel):
    nb = REL_BUCKETS // 2
    max_exact = nb // 2
    ret = (rel > 0).astype(jnp.int32) * nb
    n = jnp.abs(rel)
    nf = jnp.maximum(n, 1).astype(jnp.float32)
    large = max_exact + (jnp.log(nf / max_exact) / math.log(REL_MAX_DIST / max_exact)
                         * (nb - max_exact)).astype(jnp.int32)
    large = jnp.minimum(large, nb - 1)
    return ret + jnp.where(n < max_exact, n, large)


def _bias_table(rel_bias, tq):
    e = jnp.arange(5, dtype=jnp.int32)[:, None, None]
    r = jnp.arange(tq, dtype=jnp.int32)[None, :, None]
    c = jnp.arange(tq, dtype=jnp.int32)[None, None, :]
    bucket = _t5_bucket(r - c + (e - 2) * tq)[None]
    rb = rel_bias.astype(F32) * LOG2E
    tab = jnp.zeros((rb.shape[1], 5, tq, tq), F32)
    for k in range(REL_BUCKETS):
        tab = jnp.where(bucket == k, rb[k][:, None, None, None], tab)
    return tab


def _pack_row_pairs(w):
    bits = lax.bitcast_convert_type(w.astype(BF16), jnp.uint16).astype(jnp.uint32)
    bits = bits.reshape(w.shape[0] // 2, 2, w.shape[1])
    return bits[:, 0] | (bits[:, 1] << 16)


def _pack_in_proj(w_in, alpha_w, alpha_b):
    qa, ka, va, qg, kg, vg, rg, lr, gl = jnp.split(
        w_in, [512, 1024, 1536, 1792, 2048, 2560, 3072, 3104], axis=1)
    dup = lambda w: jnp.concatenate([w.reshape(-1, GLA_HEADS, 1, GLA_DK)] * 2, axis=2).reshape(-1, 512)
    lr_pad = jnp.pad(lr, ((0, 0), (0, LANES - lr.shape[1])))
    w = jnp.concatenate([qa * (DA_HEAD_DIM ** -0.5 * LOG2E), ka, dup(qg) * GLA_DK ** -0.5, dup(kg),
                         vg, rg, gl, lr_pad], axis=1).astype(BF16)
    aw = jnp.zeros((LANES, GLA_HEADS, 2, GLA_DK), F32)
    for d in range(2):
        aw = aw.at[d * GLA_GATE_RANK:(d + 1) * GLA_GATE_RANK, :, d, :].set(
            alpha_w[d].reshape(GLA_GATE_RANK, GLA_HEADS, GLA_DK))
    ab = alpha_b.reshape(2, GLA_HEADS, GLA_DK).transpose(1, 0, 2).reshape(1, 512)
    return w, va.T.astype(BF16), aw.reshape(LANES, 512).astype(BF16), ab.astype(F32)


def _trunk(x, p, rel_bias, norm1_g, w_in, lambda_qk, da_norm_g, gla_alpha_w, gla_alpha_b, gla_norm_g,
           w_up_a, w_up_b, w_out, norm2_g, peer_w_q, peer_sub_keys, peer_u, peer_v, norm3_g,
           ple_w, ple_gate_w, final_norm_g, prepared):
    batch, seq, _ = x.shape
    depth = w_in.shape[0]
    xf = x.reshape(batch * seq, D_MODEL)
    row = lambda v: v.reshape(1, -1).astype(F32)
    for i in range(depth):
        w1, wvt, aw, ab, bias_tab, u_bf, vt_bf, sk = prepared[i]
        lam_init = 0.8 - 0.6 * math.exp(-0.3 * i)
        qa, ka, vat, qg, kg, vg, rgs, gates, la = _in_proj(xf, row(norm1_g[i]), w1, wvt, aw, ab)
        oa = _diff_attn(qa, ka, vat, bias_tab, lambda_qk[i].astype(F32), row(da_norm_g[i]),
                        batch, seq, lam_init)
        og = _gla(qg, kg, vg, la, rgs, row(gla_norm_g[i]), batch, seq)
        x1, h2, st = _mix(oa, og, gates, xf, w_up_a[i].astype(BF16), w_up_b[i].astype(BF16),
                          w_out[i].astype(BF16), row(norm2_g[i]), peer_w_q[i].astype(BF16), sk)
        r2, e2, n1, f1 = _peer_topk(st)
        x2 = _peer_dense(h2, u_bf, vt_bf, r2, e2, n1, f1, x1)
        xf = _ple(x2, p[i].reshape(batch * seq, PLE_DIM), row(norm3_g[i]), ple_gate_w[i].astype(BF16),
                  ple_w[i].astype(BF16), row(final_norm_g), final_norm=(i == depth - 1))
    return xf.reshape(batch, seq, D_MODEL)


def kernel(x_prompt, x_sample, p_prompt, p_sample, rel_bias, norm1_g, w_in, lambda_qk, da_norm_g,
           gla_alpha_w, gla_alpha_b, gla_norm_g, w_up_a, w_up_b, w_out, norm2_g, peer_w_q,
           peer_sub_keys, peer_u, peer_v, norm3_g, ple_w, ple_gate_w, final_norm_g):
    prepared = []
    for i in range(w_in.shape[0]):
        w1, wvt, aw, ab = _pack_in_proj(w_in[i], gla_alpha_w[i], gla_alpha_b[i])
        prepared.append((w1, wvt, aw, ab, _bias_table(rel_bias, ATT_TQ), _pack_row_pairs(peer_u[i]),
                         _pack_row_pairs(peer_v[i].T),
                         peer_sub_keys[i].reshape(2 * PEER_HEADS, PEER_N_KEYS, -1).astype(BF16)))
    shared = (rel_bias, norm1_g, w_in, lambda_qk, da_norm_g, gla_alpha_w, gla_alpha_b, gla_norm_g,
              w_up_a, w_up_b, w_out, norm2_g, peer_w_q, peer_sub_keys, peer_u, peer_v, norm3_g,
              ple_w, ple_gate_w, final_norm_g)
    y_prompt = _trunk(x_prompt, p_prompt, *shared, prepared)
    y_sample = _trunk(x_sample, p_sample, *shared, prepared)
    return (y_prompt, y_sample)
```
